```python
import jax, jax.numpy as jnp
from jax import lax
import numpy as np

D_MODEL = 4096
BATCH = 4
SEQ = 4096
DEPTH = 1

HEAD_DIM = 128
SB_HEADS = D_MODEL // 256
DIL_HEADS = D_MODEL // 512
DIL_GROUPS = ((128, 1), (512, 4), (2048, 16))
N_DIL = len(DIL_GROUPS)
D_SB = SB_HEADS * HEAD_DIM
D_DIL = DIL_HEADS * HEAD_DIM
D_FF = 4 * D_MODEL
Q_BLOCK = 128
ROPE_THETA = 10000.0
EPS = 1e-6
D_IN = 3 * D_SB + 3 * N_DIL * D_DIL + 2 * D_MODEL

kernel_name = 'hybrid_stickbreak_dilated_gated_block'


def rms_norm(x, w):
    xf = x.astype(jnp.float32)
    y = xf * lax.rsqrt(jnp.mean(xf * xf, axis=-1, keepdims=True) + EPS)
    return (y * w.astype(jnp.float32)).astype(x.dtype)


def rope(x, pos):
    half = HEAD_DIM // 2
    inv_freq = ROPE_THETA ** (-jnp.arange(half, dtype=jnp.float32) / half)
    ang = pos.astype(jnp.float32)[..., None] * inv_freq
    ang = jnp.expand_dims(ang, tuple(range(2, x.ndim - 1)))
    cos, sin = jnp.cos(ang), jnp.sin(ang)
    xf = x.astype(jnp.float32)
    x1, x2 = xf[..., :half], xf[..., half:]
    return jnp.concatenate([x1 * cos - x2 * sin, x2 * cos + x1 * sin], axis=-1).astype(x.dtype)


def to_blocks(t):
    b, s = t.shape[:2]
    return jnp.moveaxis(t.reshape(b, s // Q_BLOCK, Q_BLOCK, *t.shape[2:]), 1, 0)


def from_blocks(t):
    t = jnp.moveaxis(t, 0, 1)
    return t.reshape(t.shape[0], t.shape[1] * t.shape[2], *t.shape[3:])


def stick_breaking_attention(q, k, v):
    s_len, d = q.shape[1], q.shape[-1]
    scale = d ** -0.5
    key_pos = jnp.arange(s_len)

    def block(args):
        i, q_blk = args
        z = jnp.einsum('bqhd,bshd->bhqs', q_blk, k).astype(jnp.float32) * scale
        q_pos = i * Q_BLOCK + jnp.arange(Q_BLOCK)
        causal = key_pos[None, :] < q_pos[:, None]
        log_rest = jnp.where(causal, jax.nn.log_sigmoid(-z), 0.0)
        suffix = lax.cumsum(log_rest, axis=3, reverse=True) - log_rest
        a = jnp.where(causal, jnp.exp(jax.nn.log_sigmoid(z) + suffix), 0.0)
        return jnp.einsum('bhqs,bshd->bqhd', a.astype(v.dtype), v)

    out = lax.map(block, (jnp.arange(s_len // Q_BLOCK), to_blocks(q)))
    return from_blocks(out)


def dilated_attention(q, k, v):
    s_len = q.shape[1]
    scale = HEAD_DIM ** -0.5
    outs, lses = [], []
    for g, (window, dilation) in enumerate(DIL_GROUPS):
        n_taps = window // dilation + 1
        kg, vg = k[:, :, g], v[:, :, g]

        def block(args, kg=kg, vg=vg, dilation=dilation, n_taps=n_taps):
            i, q_blk = args
            q_pos = i * Q_BLOCK + jnp.arange(Q_BLOCK)
            idx = q_pos[:, None] - dilation * jnp.arange(n_taps)[None, :]
            valid = idx >= 0
            idx = jnp.maximum(idx, 0)
            k_sel = jnp.take(kg, idx, axis=1)
            v_sel = jnp.take(vg, idx, axis=1)
            sc = jnp.einsum('bqhd,bqkhd->bhqk', q_blk, k_sel).astype(jnp.float32) * scale
            sc = jnp.where(valid[None, None], sc, -jnp.inf)
            lse = jax.nn.logsumexp(sc, axis=-1)
            p = jnp.exp(sc - lse[..., None])
            o = jnp.einsum('bhqk,bqkhd->bqhd', p.astype(vg.dtype), v_sel)
            return o, jnp.moveaxis(lse, 1, 2)

        o_g, lse_g = lax.map(block, (jnp.arange(s_len // Q_BLOCK), to_blocks(q[:, :, g])))
        outs.append(from_blocks(o_g))
        lses.append(from_blocks(lse_g))
    o = jnp.stack(outs, axis=2)
    w = jax.nn.softmax(jnp.stack(lses, axis=2), axis=2)
    return jnp.sum(o * w[..., None].astype(o.dtype), axis=2)


def token_mixer(h, positions, w_in, w_o_sb, w_o_dil, w_out):
    b, s, _ = h.shape
    proj = h @ w_in
    splits = np.cumsum([D_SB, D_SB, D_SB, N_DIL * D_DIL, N_DIL * D_DIL, N_DIL * D_DIL, D_MODEL]).tolist()
    qa, ka, va, qb, kb, vb, ga, gb = jnp.split(proj, splits, axis=-1)
    sb_shape = (b, s, SB_HEADS, HEAD_DIM)
    dil_shape = (b, s, N_DIL, DIL_HEADS, HEAD_DIM)
    y_a = stick_breaking_attention(qa.reshape(sb_shape), ka.reshape(sb_shape), va.reshape(sb_shape))
    y_a = y_a.reshape(b, s, D_SB) @ w_o_sb
    qb = rope(qb.reshape(dil_shape), positions)
    kb = rope(kb.reshape(dil_shape), positions)
    y_b = dilated_attention(qb, kb, vb.reshape(dil_shape))
    y_b = y_b.reshape(b, s, D_DIL) @ w_o_dil
    merged = jax.nn.sigmoid(ga) * y_a + jax.nn.sigmoid(gb) * y_b
    return merged @ w_out


def setup_inputs(seed: int = 0) -> dict:
    key = jax.random.key(seed)
    ks = jax.random.split(key, 14)
    f32 = jnp.float32
    nrm = lambda k, shape, fan_in, gain=1.0: (jax.random.normal(k, shape, f32) * (gain * fan_in ** -0.5)).astype(f32)
    return {
        'x': jax.random.normal(ks[0], (BATCH, SEQ, D_MODEL), f32),
        'c': jax.random.normal(ks[1], (BATCH, D_MODEL), f32),
        'positions': jnp.broadcast_to(jnp.arange(SEQ, dtype=jnp.int32), (BATCH, SEQ)),
        'ada_w': nrm(ks[2], (DEPTH, D_MODEL, 6 * D_MODEL), D_MODEL, 0.5),
        'ada_b': 0.01 * jax.random.normal(ks[3], (DEPTH, 6 * D_MODEL), f32),
        'norm_mix_w': 1.0 + 0.02 * jax.random.normal(ks[4], (DEPTH, D_MODEL), f32),
        'w_in': nrm(ks[5], (DEPTH, D_MODEL, D_IN), D_MODEL),
        'w_o_sb': nrm(ks[6], (DEPTH, D_SB, D_MODEL), D_SB),
        'w_o_dil': nrm(ks[7], (DEPTH, D_DIL, D_MODEL), D_DIL),
        'w_out': nrm(ks[8], (DEPTH, D_MODEL, D_MODEL), D_MODEL),
        'norm_mlp_w': 1.0 + 0.02 * jax.random.normal(ks[9], (DEPTH, D_MODEL), f32),
        'w_ff1': nrm(ks[10], (DEPTH, D_MODEL, D_FF), D_MODEL),
        'w_ff2': nrm(ks[11], (DEPTH, D_FF, D_MODEL), D_FF),
        'norm_final_w': 1.0 + 0.02 * jax.random.normal(ks[12], (D_MODEL,), f32),
    }


def reference(x, c, positions, ada_w, ada_b, norm_mix_w, w_in, w_o_sb, w_o_dil, w_out,
              norm_mlp_w, w_ff1, w_ff2, norm_final_w):
    for l in range(DEPTH):
        mod = jax.nn.silu(c) @ ada_w[l] + ada_b[l]
        sh1, sc1, g1, sh2, sc2, g2 = [m[:, None, :] for m in jnp.split(mod, 6, axis=-1)]
        h = rms_norm(x, norm_mix_w[l]) * (1 + sc1) + sh1
        x = x + g1 * token_mixer(h, positions, w_in[l], w_o_sb[l], w_o_dil[l], w_out[l])
        h = rms_norm(x, norm_mlp_w[l]) * (1 + sc2) + sh2
        x = x + g2 * (jnp.square(jax.nn.relu(h @ w_ff1[l])) @ w_ff2[l])
    return rms_norm(x, norm_final_w)
```

```python
import functools

import jax
import jax.numpy as jnp
import numpy as np
from jax import lax
from jax.experimental import pallas as pl
from jax.experimental.pallas import tpu as pltpu

F32 = jnp.float32
BF16 = jnp.bfloat16

HEAD_DIM = 128
LANES = 128
DIL_GROUPS = ((128, 1), (512, 4), (2048, 16))
N_DIL = len(DIL_GROUPS)
ROPE_THETA = 10000.0
EPS = 1e-6
VMEM_LIMIT = 56 * 1024 * 1024


def _params(*sem):
    return pltpu.CompilerParams(dimension_semantics=sem, vmem_limit_bytes=VMEM_LIMIT)


def _ada_kernel(c_ref, w_ref, b_ref, o_ref):
    c = c_ref[...]
    s = c * (1.0 / (1.0 + jnp.exp(-c)))
    o_ref[...] = jnp.dot(s.astype(BF16), w_ref[...].astype(BF16),
                         preferred_element_type=F32) + b_ref[...]


def _ada_modulation(c, w, b, *, bn=512):
    bsz, d = c.shape
    n = w.shape[1]
    rows = 8
    c_pad = jnp.zeros((rows, d), F32).at[:bsz].set(c)
    out = pl.pallas_call(
        _ada_kernel,
        grid=(n // bn,),
        in_specs=[pl.BlockSpec((rows, d), lambda j: (0, 0)),
                  pl.BlockSpec((d, bn), lambda j: (0, j)),
                  pl.BlockSpec((1, bn), lambda j: (0, j))],
        out_specs=pl.BlockSpec((rows, bn), lambda j: (0, j)),
        out_shape=jax.ShapeDtypeStruct((rows, n), F32),
        compiler_params=_params("arbitrary"),
        name="ada_modulation",
    )(c_pad, w, b.reshape(1, n))
    return out[:bsz]


def _norm_mod_kernel(x_ref, w_ref, sc_ref, sh_ref, o_ref):
    x = x_ref[...]
    inv = lax.rsqrt(jnp.mean(x * x, axis=-1, keepdims=True) + EPS)
    mul = w_ref[...] * (1.0 + sc_ref[...])
    o_ref[...] = (x * inv * mul + sh_ref[...]).astype(o_ref.dtype)


def _norm_mod(x2d, w, sc, sh, seq, *, tm=256):
    n, d = x2d.shape
    bsz = sc.shape[0]
    per_seq = seq // tm
    return pl.pallas_call(
        _norm_mod_kernel,
        grid=(n // tm,),
        in_specs=[pl.BlockSpec((tm, d), lambda i: (i, 0)),
                  pl.BlockSpec((1, d), lambda i: (0, 0)),
                  pl.BlockSpec((None, 1, d), lambda i: (i // per_seq, 0, 0)),
                  pl.BlockSpec((None, 1, d), lambda i: (i // per_seq, 0, 0))],
        out_specs=pl.BlockSpec((tm, d), lambda i: (i, 0)),
        out_shape=jax.ShapeDtypeStruct((n, d), BF16),
        compiler_params=_params("arbitrary"),
        name="norm_modulate",
    )(x2d, w.reshape(1, d), sc.reshape(bsz, 1, d), sh.reshape(bsz, 1, d))


def _final_norm_kernel(x_ref, w_ref, o_ref):
    x = x_ref[...]
    inv = lax.rsqrt(jnp.mean(x * x, axis=-1, keepdims=True) + EPS)
    o_ref[...] = x * inv * w_ref[...]


def _final_norm(x2d, w, *, tm=256):
    n, d = x2d.shape
    return pl.pallas_call(
        _final_norm_kernel,
        grid=(n // tm,),
        in_specs=[pl.BlockSpec((tm, d), lambda i: (i, 0)),
                  pl.BlockSpec((1, d), lambda i: (0, 0))],
        out_specs=pl.BlockSpec((tm, d), lambda i: (i, 0)),
        out_shape=jax.ShapeDtypeStruct((n, d), F32),
        compiler_params=_params("arbitrary"),
        name="final_norm",
    )(x2d, w.reshape(1, d))


def _rope_table_kernel(pos_ref, invf_ref, sign_ref, cos_ref, sin_ref):
    ang = pos_ref[...].astype(F32) * invf_ref[...]
    cos_ref[...] = jnp.cos(ang)
    sin_ref[...] = jnp.sin(ang) * sign_ref[...]


def _rope_tables(positions, *, tm=2048):
    n = positions.size
    half = HEAD_DIM // 2
    inv_freq = ROPE_THETA ** (-jnp.arange(half, dtype=F32) / half)
    invf = jnp.concatenate([inv_freq, inv_freq]).reshape(1, HEAD_DIM)
    sign = jnp.concatenate([-jnp.ones((half,), F32), jnp.ones((half,), F32)]).reshape(1, HEAD_DIM)
    tm = min(tm, n)
    return pl.pallas_call(
        _rope_table_kernel,
        grid=(n // tm,),
        in_specs=[pl.BlockSpec((tm, 1), lambda i: (i, 0)),
                  pl.BlockSpec((1, HEAD_DIM), lambda i: (0, 0)),
                  pl.BlockSpec((1, HEAD_DIM), lambda i: (0, 0))],
        out_specs=[pl.BlockSpec((tm, HEAD_DIM), lambda i: (i, 0)),
                   pl.BlockSpec((tm, HEAD_DIM), lambda i: (i, 0))],
        out_shape=[jax.ShapeDtypeStruct((n, HEAD_DIM), F32),
                   jax.ShapeDtypeStruct((n, HEAD_DIM), F32)],
        compiler_params=_params("arbitrary"),
        name="rope_tables",
    )(positions.reshape(n, 1), invf, sign)


def _inproj_kernel(a_ref, w_ref, cos_ref, sin_ref, o_ref, *, rope_lo, rope_hi):
    j = pl.program_id(1)
    acc = jnp.dot(a_ref[...], w_ref[...], preferred_element_type=F32)
    is_rope = jnp.logical_and(j >= rope_lo, j < rope_hi)

    @pl.when(is_rope)
    def _():
        cos = cos_ref[...]
        sin = sin_ref[...]
        for h in range(acc.shape[1] // HEAD_DIM):
            cols = slice(h * HEAD_DIM, (h + 1) * HEAD_DIM)
            xh = acc[:, cols]
            o_ref[:, cols] = (xh * cos + pltpu.roll(xh, HEAD_DIM // 2, 1) * sin).astype(o_ref.dtype)

    @pl.when(jnp.logical_not(is_rope))
    def _():
        o_ref[...] = acc.astype(o_ref.dtype)


def _in_projection(h, w, cos, sin, rope_cols, *, bm=1024, bn=1024):
    m, k = h.shape
    n = w.shape[1]
    kern = functools.partial(_inproj_kernel, rope_lo=rope_cols[0] // bn, rope_hi=rope_cols[1] // bn)
    return pl.pallas_call(
        kern,
        grid=(m // bm, n // bn),
        in_specs=[pl.BlockSpec((bm, k), lambda i, j: (i, 0)),
                  pl.BlockSpec((k, bn), lambda i, j: (0, j)),
                  pl.BlockSpec((bm, HEAD_DIM), lambda i, j: (i, 0)),
                  pl.BlockSpec((bm, HEAD_DIM), lambda i, j: (i, 0))],
        out_specs=pl.BlockSpec((bm, bn), lambda i, j: (i, j)),
        out_shape=jax.ShapeDtypeStruct((m, n), BF16),
        compiler_params=_params("arbitrary", "arbitrary"),
        name="in_projection",
    )(h, w, cos, sin)


def _merge_kernel(ya_ref, wa_ref, yb_ref, wb_ref, ga_ref, gb_ref, o_ref):
    a = jnp.dot(ya_ref[...], wa_ref[...], preferred_element_type=F32)
    b = jnp.dot(yb_ref[...], wb_ref[...], preferred_element_type=F32)
    ga = ga_ref[...].astype(F32)
    gb = gb_ref[...].astype(F32)
    o_ref[...] = (a / (1.0 + jnp.exp(-ga)) + b / (1.0 + jnp.exp(-gb))).astype(o_ref.dtype)


def _gated_merge(ya, wa, yb, wb, proj, ga_col, gb_col, *, bm=1024, bn=1024):
    m = ya.shape[0]
    n = wa.shape[1]
    ga_blk, gb_blk = ga_col // bn, gb_col // bn
    return pl.pallas_call(
        _merge_kernel,
        grid=(m // bm, n // bn),
        in_specs=[pl.BlockSpec((bm, ya.shape[1]), lambda i, j: (i, 0)),
                  pl.BlockSpec((wa.shape[0], bn), lambda i, j: (0, j)),
                  pl.BlockSpec((bm, yb.shape[1]), lambda i, j: (i, 0)),
                  pl.BlockSpec((wb.shape[0], bn), lambda i, j: (0, j)),
                  pl.BlockSpec((bm, bn), lambda i, j: (i, ga_blk + j)),
                  pl.BlockSpec((bm, bn), lambda i, j: (i, gb_blk + j))],
        out_specs=pl.BlockSpec((bm, bn), lambda i, j: (i, j)),
        out_shape=jax.ShapeDtypeStruct((m, n), BF16),
        compiler_params=_params("arbitrary", "arbitrary"),
        name="gated_merge",
    )(ya, wa, yb, wb, proj, proj)


def _residual_mm_kernel(a_ref, w_ref, x_ref, g_ref, o_ref):
    acc = jnp.dot(a_ref[...], w_ref[...], preferred_element_type=F32)
    o_ref[...] = x_ref[...] + g_ref[...] * acc


def _residual_matmul(a, w, x2d, gate, seq, *, bm=1024, bn=1024):
    m, k = a.shape
    n = w.shape[1]
    bsz = gate.shape[0]
    per_seq = seq // bm
    return pl.pallas_call(
        _residual_mm_kernel,
        grid=(m // bm, n // bn),
        in_specs=[pl.BlockSpec((bm, k), lambda i, j: (i, 0)),
                  pl.BlockSpec((k, bn), lambda i, j: (0, j)),
                  pl.BlockSpec((bm, bn), lambda i, j: (i, j)),
                  pl.BlockSpec((None, 1, bn), lambda i, j: (i // per_seq, 0, j))],
        out_specs=pl.BlockSpec((bm, bn), lambda i, j: (i, j)),
        out_shape=jax.ShapeDtypeStruct((m, n), F32),
        compiler_params=_params("arbitrary", "arbitrary"),
        name="out_projection",
    )(a, w, x2d, gate.reshape(bsz, 1, n))


def _relu2_mm_kernel(a_ref, w_ref, o_ref):
    acc = jnp.dot(a_ref[...], w_ref[...], preferred_element_type=F32)
    r = jnp.maximum(acc, 0.0)
    o_ref[...] = (r * r).astype(o_ref.dtype)


def _relu2_matmul(a, w, *, bm=1024, bn=1024):
    m, k = a.shape
    n = w.shape[1]
    return pl.pallas_call(
        _relu2_mm_kernel,
        grid=(m // bm, n // bn),
        in_specs=[pl.BlockSpec((bm, k), lambda i, j: (i, 0)),
                  pl.BlockSpec((k, bn), lambda i, j: (0, j))],
        out_specs=pl.BlockSpec((bm, bn), lambda i, j: (i, j)),
        out_shape=jax.ShapeDtypeStruct((m, n), BF16),
        compiler_params=_params("arbitrary", "arbitrary"),
        name="mlp_up",
    )(a, w)


def _residual_mm_ksplit_kernel(a_ref, w_ref, x_ref, g_ref, o_ref, acc_ref):
    kk = pl.program_id(2)

    @pl.when(kk == 0)
    def _():
        acc_ref[...] = jnp.zeros_like(acc_ref)

    acc_ref[...] += jnp.dot(a_ref[...], w_ref[...], preferred_element_type=F32)

    @pl.when(kk == pl.num_programs(2) - 1)
    def _():
        o_ref[...] = x_ref[...] + g_ref[...] * acc_ref[...]


def _residual_matmul_ksplit(a, w, x2d, gate, seq, *, bm=1024, bn=1024, bk=2048):
    m, k = a.shape
    n = w.shape[1]
    bsz = gate.shape[0]
    per_seq = seq // bm
    return pl.pallas_call(
        _residual_mm_ksplit_kernel,
        grid=(m // bm, n // bn, k // bk),
        in_specs=[pl.BlockSpec((bm, bk), lambda i, j, kk: (i, kk)),
                  pl.BlockSpec((bk, bn), lambda i, j, kk: (kk, j)),
                  pl.BlockSpec((bm, bn), lambda i, j, kk: (i, j)),
                  pl.BlockSpec((None, 1, bn), lambda i, j, kk: (i // per_seq, 0, j))],
        out_specs=pl.BlockSpec((bm, bn), lambda i, j, kk: (i, j)),
        out_shape=jax.ShapeDtypeStruct((m, n), F32),
        scratch_shapes=[pltpu.VMEM((bm, bn), F32)],
        compiler_params=_params("arbitrary", "arbitrary", "arbitrary"),
        name="mlp_down",
    )(a, w, x2d, gate.reshape(bsz, 1, n))


def _suffix_sum_matrix():
    r = np.arange(2 * LANES)[:, None] % LANES
    c = np.arange(2 * LANES)[None, :]
    return jnp.asarray(np.where(c < LANES, r > c, True), dtype=BF16)


def _sb_kernel(q_ref, k_ref, v_ref, ut_ref, o_ref, acc_ref, carry_ref, *, tile, scale):
    seq = q_ref.shape[0]
    n_sub = tile // LANES

    def q_block(iq, _):
        q0 = pl.multiple_of(iq * tile, tile)
        q = q_ref[pl.ds(q0, tile), :]
        acc_ref[...] = jnp.zeros_like(acc_ref)
        carry_ref[...] = jnp.zeros_like(carry_ref)

        def visit(k0, diagonal):
            k = k_ref[pl.ds(k0, tile), :]
            v = v_ref[pl.ds(k0, tile), :]
            z = lax.dot_general(q, k, (((1,), (1,)), ((), ())), preferred_element_type=F32) * scale
            softplus = jnp.log1p(jnp.exp(-jnp.abs(z)))
            log_rest = jnp.minimum(-z, 0.0) - softplus
            log_beta = jnp.minimum(z, 0.0) - softplus
            if diagonal:
                row = lax.broadcasted_iota(jnp.int32, (tile, tile), 0)
                col = lax.broadcasted_iota(jnp.int32, (tile, tile), 1)
                causal = col < row
                log_rest = jnp.where(causal, log_rest, 0.0)
            carry = carry_ref[...]
            parts = [None] * n_sub
            for sb in reversed(range(n_sub)):
                cols = slice(sb * LANES, (sb + 1) * LANES)
                lr = log_rest[:, cols]
                hi = lr.astype(BF16)
                lo = (lr - hi.astype(F32)).astype(BF16)
                cs = jnp.dot(jnp.concatenate([hi, lo], axis=1), ut_ref[...],
                             preferred_element_type=F32)
                a = jnp.exp(log_beta[:, cols] + cs[:, :LANES] + carry)
                if diagonal:
                    a = jnp.where(causal[:, cols], a, 0.0)
                parts[sb] = a.astype(BF16)
                carry = carry + cs[:, LANES:]
            carry_ref[...] = carry
            acc_ref[...] += jnp.dot(jnp.concatenate(parts, axis=1), v, preferred_element_type=F32)

        visit(q0, True)

        def left(t, _):
            visit(pl.multiple_of(q0 - (t + 1) * tile, tile), False)
            return 0

        lax.fori_loop(0, iq, left, 0)
        o_ref[pl.ds(q0, tile), :] = acc_ref[...].astype(o_ref.dtype)
        return 0

    lax.fori_loop(0, seq // tile, q_block, 0)


def _stick_breaking(proj, bsz, seq, n_heads, q_col, k_col, v_col, *, tile=256):
    hb = HEAD_DIM
    qb, kb, vb = q_col // hb, k_col // hb, v_col // hb
    kern = functools.partial(_sb_kernel, tile=tile, scale=HEAD_DIM ** -0.5)
    return pl.pallas_call(
        kern,
        grid=(bsz, n_heads),
        in_specs=[pl.BlockSpec((seq, hb), lambda b, h: (b, qb + h)),
                  pl.BlockSpec((seq, hb), lambda b, h: (b, kb + h)),
                  pl.BlockSpec((seq, hb), lambda b, h: (b, vb + h)),
                  pl.BlockSpec((2 * LANES, 2 * LANES), lambda b, h: (0, 0))],
        out_specs=pl.BlockSpec((seq, hb), lambda b, h: (b, h)),
        out_shape=jax.ShapeDtypeStruct((bsz * seq, n_heads * hb), BF16),
        scratch_shapes=[pltpu.VMEM((tile, hb), F32), pltpu.VMEM((tile, LANES), F32)],
        compiler_params=_params("arbitrary", "arbitrary"),
        name="stick_breaking",
    )(proj, proj, proj, _suffix_sum_matrix())


DIL_BLOCK = 128


def _dil_heads(q_ref, kp_ref, kc_ref, vp_ref, vc_ref, first, n_heads, scale):
    blk = DIL_BLOCK
    row = lax.broadcasted_iota(jnp.int32, (blk, 2 * blk), 0)
    col = lax.broadcasted_iota(jnp.int32, (blk, 2 * blk), 1)
    valid = jnp.logical_and(col >= row, col <= row + blk)
    valid = jnp.logical_and(valid, jnp.logical_or(col >= blk, jnp.logical_not(first)))
    for h in range(n_heads):
        cols = slice(h * HEAD_DIM, (h + 1) * HEAD_DIM)
        q = q_ref[:, cols]
        k = jnp.concatenate([kp_ref[:, cols], kc_ref[:, cols]], axis=0)
        v = jnp.concatenate([vp_ref[:, cols], vc_ref[:, cols]], axis=0)
        s = lax.dot_general(q, k, (((1,), (1,)), ((), ())), preferred_element_type=F32) * scale
        s = jnp.where(valid, s, -jnp.inf)
        m = jnp.max(s, axis=1, keepdims=True)
        p = jnp.exp(s - m)
        l = jnp.sum(p, axis=1, keepdims=True)
        o = jnp.dot(p.astype(BF16), v, preferred_element_type=F32) / l
        yield h, cols, o, m + jnp.log(l)


def _dil_group_kernel(q_ref, kp_ref, kc_ref, vp_ref, vc_ref, o_ref, lse_ref, *, n_heads, scale):
    first = pl.program_id(2) == 0
    lane = lax.broadcasted_iota(jnp.int32, lse_ref.shape, 1)
    lse_blk = jnp.zeros(lse_ref.shape, F32)
    for h, cols, o, lse in _dil_heads(q_ref, kp_ref, kc_ref, vp_ref, vc_ref, first, n_heads, scale):
        o_ref[:, cols] = o
        lse_blk = jnp.where(lane == h, lse, lse_blk)
    lse_ref[...] = lse_blk


def _dil_combine_kernel(q_ref, kp_ref, kc_ref, vp_ref, vc_ref, o2_ref, l2_ref, o3_ref, l3_ref,
                        y_ref, *, n_heads, scale):
    first = pl.program_id(1) == 0
    for h, cols, o1, lse1 in _dil_heads(q_ref, kp_ref, kc_ref, vp_ref, vc_ref, first, n_heads, scale):
        lse2 = l2_ref[:, h:h + 1]
        lse3 = l3_ref[:, h:h + 1]
        mx = jnp.maximum(jnp.maximum(lse1, lse2), lse3)
        w1, w2, w3 = jnp.exp(lse1 - mx), jnp.exp(lse2 - mx), jnp.exp(lse3 - mx)
        y = (w1 * o1 + w2 * o2_ref[:, cols] + w3 * o3_ref[:, cols]) / (w1 + w2 + w3)
        y_ref[:, cols] = y.astype(y_ref.dtype)


def _dilated_attention(proj, bsz, seq, n_heads, q_col, k_col, v_col):
    n, d_in = proj.shape
    gw = n_heads * HEAD_DIM
    blk = DIL_BLOCK
    scale = HEAD_DIM ** -0.5
    partial = []
    for g in range(N_DIL - 1, 0, -1):
        dil = DIL_GROUPS[g][1]
        sub = seq // dil
        nblk = sub // blk
        view = proj.reshape(n // dil, dil * d_in)
        per_c = d_in // gw
        qo, ko, vo = (q_col // gw + g), (k_col // gw + g), (v_col // gw + g)

        def cur(off):
            return lambda b, c, a: (b * nblk + a, c * per_c + off)

        def prev(off):
            return lambda b, c, a: (b * nblk + jnp.maximum(a - 1, 0), c * per_c + off)

        o_g, lse_g = pl.pallas_call(
            functools.partial(_dil_group_kernel, n_heads=n_heads, scale=scale),
            grid=(bsz, dil, nblk),
            in_specs=[pl.BlockSpec((blk, gw), cur(qo)),
                      pl.BlockSpec((blk, gw), prev(ko)),
                      pl.BlockSpec((blk, gw), cur(ko)),
                      pl.BlockSpec((blk, gw), prev(vo)),
                      pl.BlockSpec((blk, gw), cur(vo))],
            out_specs=[pl.BlockSpec((blk, gw), lambda b, c, a: (b * nblk + a, c)),
                       pl.BlockSpec((blk, LANES), lambda b, c, a: (b * nblk + a, c))],
            out_shape=[jax.ShapeDtypeStruct((n // dil, dil * gw), F32),
                       jax.ShapeDtypeStruct((n // dil, dil * LANES), F32)],
            compiler_params=_params("arbitrary", "arbitrary", "arbitrary"),
            name=f"dilated_group{g}",
        )(view, view, view, view, view)
        partial.append((o_g.reshape(n, gw), lse_g.reshape(n, LANES)))

    (o3, l3), (o2, l2) = partial
    nblk = seq // blk
    qo, ko, vo = q_col // gw, k_col // gw, v_col // gw

    def cur(off):
        return lambda b, a: (b * nblk + a, off)

    def prev(off):
        return lambda b, a: (b * nblk + jnp.maximum(a - 1, 0), off)

    same = lambda b, a: (b * nblk + a, 0)
    return pl.pallas_call(
        functools.partial(_dil_combine_kernel, n_heads=n_heads, scale=scale),
        grid=(bsz, nblk),
        in_specs=[pl.BlockSpec((blk, gw), cur(qo)),
                  pl.BlockSpec((blk, gw), prev(ko)),
                  pl.BlockSpec((blk, gw), cur(ko)),
                  pl.BlockSpec((blk, gw), prev(vo)),
                  pl.BlockSpec((blk, gw), cur(vo)),
                  pl.BlockSpec((blk, gw), same),
                  pl.BlockSpec((blk, LANES), same),
                  pl.BlockSpec((blk, gw), same),
                  pl.BlockSpec((blk, LANES), same)],
        out_specs=pl.BlockSpec((blk, gw), same),
        out_shape=jax.ShapeDtypeStruct((n, gw), BF16),
        compiler_params=_params("arbitrary", "arbitrary"),
        name="dilated_group0_combine",
    )(proj, proj, proj, proj, proj, o2, l2, o3, l3)


def kernel(x, c, positions, ada_w, ada_b, norm_mix_w, w_in, w_o_sb, w_o_dil, w_out,
           norm_mlp_w, w_ff1, w_ff2, norm_final_w):
    bsz, seq, d = x.shape
    n = bsz * seq
    d_sb = w_o_sb.shape[1]
    d_dil = w_o_dil.shape[1]
    sb_heads = d_sb // HEAD_DIM
    dil_heads = d_dil // HEAD_DIM
    q_sb, k_sb, v_sb = 0, d_sb, 2 * d_sb
    q_dil = 3 * d_sb
    k_dil = q_dil + N_DIL * d_dil
    v_dil = k_dil + N_DIL * d_dil
    g_sb = v_dil + N_DIL * d_dil
    g_dil = g_sb + d

    cos, sin = _rope_tables(positions)
    xf = x.reshape(n, d)
    for l in range(ada_w.shape[0]):
        mod = _ada_modulation(c, ada_w[l], ada_b[l])
        sh1, sc1, g1, sh2, sc2, g2 = [mod[:, i * d:(i + 1) * d] for i in range(6)]

        h = _norm_mod(xf, norm_mix_w[l], sc1, sh1, seq)
        proj = _in_projection(h, w_in[l].astype(BF16), cos, sin, (q_dil, v_dil))
        y_a = _stick_breaking(proj, bsz, seq, sb_heads, q_sb, k_sb, v_sb)
        y_b = _dilated_attention(proj, bsz, seq, dil_heads, q_dil, k_dil, v_dil)
        merged = _gated_merge(y_a, w_o_sb[l].astype(BF16), y_b, w_o_dil[l].astype(BF16),
                              proj, g_sb, g_dil)
        xf = _residual_matmul(merged, w_out[l].astype(BF16), xf, g1, seq)

        h = _norm_mod(xf, norm_mlp_w[l], sc2, sh2, seq)
        u = _relu2_matmul(h, w_ff1[l].astype(BF16))
        xf = _residual_matmul_ksplit(u, w_ff2[l].astype(BF16), xf, g2, seq)
    return _final_norm(xf, norm_final_w).reshape(bsz, seq, d)
```

```python
import functools

import jax
import jax.numpy as jnp
import numpy as np
from jax import lax
from jax.experimental import pallas as pl
from jax.experimental.pallas import tpu as pltpu

F32 = jnp.float32
BF16 = jnp.bfloat16

HEAD_DIM = 128
LANES = 128
DIL_GROUPS = ((128, 1), (512, 4), (2048, 16))
N_DIL = len(DIL_GROUPS)
ROPE_THETA = 10000.0
EPS = 1e-6
LOG2_E = 1.4426950408889634
VMEM_LIMIT = 56 * 1024 * 1024


def _params(*sem):
    return pltpu.CompilerParams(dimension_semantics=sem, vmem_limit_bytes=VMEM_LIMIT)


def _ada_kernel(c_ref, w_ref, b_ref, o_ref):
    c = c_ref[...]
    s = c * (1.0 / (1.0 + jnp.exp(-c)))
    o_ref[...] = jnp.dot(s.astype(BF16), w_ref[...].astype(BF16),
                         preferred_element_type=F32) + b_ref[...]


def _ada_modulation(c, w, b, *, bn=512):
    bsz, d = c.shape
    n = w.shape[1]
    rows = 8
    c_pad = jnp.zeros((rows, d), F32).at[:bsz].set(c)
    out = pl.pallas_call(
        _ada_kernel,
        grid=(n // bn,),
        in_specs=[pl.BlockSpec((rows, d), lambda j: (0, 0)),
                  pl.BlockSpec((d, bn), lambda j: (0, j)),
                  pl.BlockSpec((1, bn), lambda j: (0, j))],
        out_specs=pl.BlockSpec((rows, bn), lambda j: (0, j)),
        out_shape=jax.ShapeDtypeStruct((rows, n), F32),
        compiler_params=_params("arbitrary"),
        name="ada_modulation",
    )(c_pad, w, b.reshape(1, n))
    return out[:bsz]


def _norm_mod_kernel(x_ref, w_ref, sc_ref, sh_ref, o_ref):
    x = x_ref[...]
    inv = lax.rsqrt(jnp.mean(x * x, axis=-1, keepdims=True) + EPS)
    mul = w_ref[...] * (1.0 + sc_ref[...])
    o_ref[...] = (x * inv * mul + sh_ref[...]).astype(o_ref.dtype)


def _norm_mod(x2d, w, sc, sh, seq, *, tm=256):
    n, d = x2d.shape
    bsz = sc.shape[0]
    per_seq = seq // tm
    return pl.pallas_call(
        _norm_mod_kernel,
        grid=(n // tm,),
        in_specs=[pl.BlockSpec((tm, d), lambda i: (i, 0)),
                  pl.BlockSpec((1, d), lambda i: (0, 0)),
                  pl.BlockSpec((None, 1, d), lambda i: (i // per_seq, 0, 0)),
                  pl.BlockSpec((None, 1, d), lambda i: (i // per_seq, 0, 0))],
        out_specs=pl.BlockSpec((tm, d), lambda i: (i, 0)),
        out_shape=jax.ShapeDtypeStruct((n, d), BF16),
        compiler_params=_params("arbitrary"),
        name="norm_modulate",
    )(x2d, w.reshape(1, d), sc.reshape(bsz, 1, d), sh.reshape(bsz, 1, d))


def _final_norm_kernel(x_ref, w_ref, o_ref):
    x = x_ref[...]
    inv = lax.rsqrt(jnp.mean(x * x, axis=-1, keepdims=True) + EPS)
    o_ref[...] = x * inv * w_ref[...]


def _final_norm(x2d, w, *, tm=256):
    n, d = x2d.shape
    return pl.pallas_call(
        _final_norm_kernel,
        grid=(n // tm,),
        in_specs=[pl.BlockSpec((tm, d), lambda i: (i, 0)),
                  pl.BlockSpec((1, d), lambda i: (0, 0))],
        out_specs=pl.BlockSpec((tm, d), lambda i: (i, 0)),
        out_shape=jax.ShapeDtypeStruct((n, d), F32),
        compiler_params=_params("arbitrary"),
        name="final_norm",
    )(x2d, w.reshape(1, d))


def _rope_table_kernel(pos_ref, invf_ref, sign_ref, cos_ref, sin_ref):
    ang = pos_ref[...].astype(F32) * invf_ref[...]
    cos_ref[...] = jnp.cos(ang)
    sin_ref[...] = jnp.sin(ang) * sign_ref[...]


def _rope_tables(positions, *, tm=2048):
    n = positions.size
    half = HEAD_DIM // 2
    inv_freq = ROPE_THETA ** (-jnp.arange(half, dtype=F32) / half)
    invf = jnp.concatenate([inv_freq, inv_freq]).reshape(1, HEAD_DIM)
    sign = jnp.concatenate([-jnp.ones((half,), F32), jnp.ones((half,), F32)]).reshape(1, HEAD_DIM)
    tm = min(tm, n)
    return pl.pallas_call(
        _rope_table_kernel,
        grid=(n // tm,),
        in_specs=[pl.BlockSpec((tm, 1), lambda i: (i, 0)),
                  pl.BlockSpec((1, HEAD_DIM), lambda i: (0, 0)),
                  pl.BlockSpec((1, HEAD_DIM), lambda i: (0, 0))],
        out_specs=[pl.BlockSpec((tm, HEAD_DIM), lambda i: (i, 0)),
                   pl.BlockSpec((tm, HEAD_DIM), lambda i: (i, 0))],
        out_shape=[jax.ShapeDtypeStruct((n, HEAD_DIM), F32),
                   jax.ShapeDtypeStruct((n, HEAD_DIM), F32)],
        compiler_params=_params("arbitrary"),
        name="rope_tables",
    )(positions.reshape(n, 1), invf, sign)


def _inproj_kernel(a_ref, w_ref, cos_ref, sin_ref, o_ref, *, rope_lo, rope_hi):
    j = pl.program_id(1)
    acc = jnp.dot(a_ref[...], w_ref[...], preferred_element_type=F32)
    is_rope = jnp.logical_and(j >= rope_lo, j < rope_hi)

    @pl.when(is_rope)
    def _():
        cos = cos_ref[...]
        sin = sin_ref[...]
        for h in range(acc.shape[1] // HEAD_DIM):
            cols = slice(h * HEAD_DIM, (h + 1) * HEAD_DIM)
            xh = acc[:, cols]
            o_ref[:, cols] = (xh * cos + pltpu.roll(xh, HEAD_DIM // 2, 1) * sin).astype(o_ref.dtype)

    @pl.when(jnp.logical_not(is_rope))
    def _():
        o_ref[...] = acc.astype(o_ref.dtype)


def _in_projection(h, w, cos, sin, rope_cols, *, bm=1024, bn=1024):
    m, k = h.shape
    n = w.shape[1]
    kern = functools.partial(_inproj_kernel, rope_lo=rope_cols[0] // bn, rope_hi=rope_cols[1] // bn)
    return pl.pallas_call(
        kern,
        grid=(m // bm, n // bn),
        in_specs=[pl.BlockSpec((bm, k), lambda i, j: (i, 0)),
                  pl.BlockSpec((k, bn), lambda i, j: (0, j)),
                  pl.BlockSpec((bm, HEAD_DIM), lambda i, j: (i, 0)),
                  pl.BlockSpec((bm, HEAD_DIM), lambda i, j: (i, 0))],
        out_specs=pl.BlockSpec((bm, bn), lambda i, j: (i, j)),
        out_shape=jax.ShapeDtypeStruct((m, n), BF16),
        compiler_params=_params("arbitrary", "arbitrary"),
        name="in_projection",
    )(h, w, cos, sin)


def _merge_kernel(ya_ref, wa_ref, yb_ref, wb_ref, ga_ref, gb_ref, o_ref):
    a = jnp.dot(ya_ref[...], wa_ref[...], preferred_element_type=F32)
    b = jnp.dot(yb_ref[...], wb_ref[...], preferred_element_type=F32)
    ga = ga_ref[...].astype(F32)
    gb = gb_ref[...].astype(F32)
    o_ref[...] = (a / (1.0 + jnp.exp(-ga)) + b / (1.0 + jnp.exp(-gb))).astype(o_ref.dtype)


def _gated_merge(ya, wa, yb, wb, proj, ga_col, gb_col, *, bm=1024, bn=1024):
    m = ya.shape[0]
    n = wa.shape[1]
    ga_blk, gb_blk = ga_col // bn, gb_col // bn
    return pl.pallas_call(
        _merge_kernel,
        grid=(m // bm, n // bn),
        in_specs=[pl.BlockSpec((bm, ya.shape[1]), lambda i, j: (i, 0)),
                  pl.BlockSpec((wa.shape[0], bn), lambda i, j: (0, j)),
                  pl.BlockSpec((bm, yb.shape[1]), lambda i, j: (i, 0)),
                  pl.BlockSpec((wb.shape[0], bn), lambda i, j: (0, j)),
                  pl.BlockSpec((bm, bn), lambda i, j: (i, ga_blk + j)),
                  pl.BlockSpec((bm, bn), lambda i, j: (i, gb_blk + j))],
        out_specs=pl.BlockSpec((bm, bn), lambda i, j: (i, j)),
        out_shape=jax.ShapeDtypeStruct((m, n), BF16),
        compiler_params=_params("arbitrary", "arbitrary"),
        name="gated_merge",
    )(ya, wa, yb, wb, proj, proj)


def _residual_mm_kernel(a_ref, w_ref, x_ref, g_ref, o_ref):
    acc = jnp.dot(a_ref[...], w_ref[...], preferred_element_type=F32)
    o_ref[...] = x_ref[...] + g_ref[...] * acc


def _residual_matmul(a, w, x2d, gate, seq, *, bm=1024, bn=1024):
    m, k = a.shape
    n = w.shape[1]
    bsz = gate.shape[0]
    per_seq = seq // bm
    return pl.pallas_call(
        _residual_mm_kernel,
        grid=(m // bm, n // bn),
        in_specs=[pl.BlockSpec((bm, k), lambda i, j: (i, 0)),
                  pl.BlockSpec((k, bn), lambda i, j: (0, j)),
                  pl.BlockSpec((bm, bn), lambda i, j: (i, j)),
                  pl.BlockSpec((None, 1, bn), lambda i, j: (i // per_seq, 0, j))],
        out_specs=pl.BlockSpec((bm, bn), lambda i, j: (i, j)),
        out_shape=jax.ShapeDtypeStruct((m, n), F32),
        compiler_params=_params("arbitrary", "arbitrary"),
        name="out_projection",
    )(a, w, x2d, gate.reshape(bsz, 1, n))


def _relu2_mm_kernel(a_ref, w_ref, o_ref):
    acc = jnp.dot(a_ref[...], w_ref[...], preferred_element_type=F32)
    r = jnp.maximum(acc, 0.0)
    o_ref[...] = (r * r).astype(o_ref.dtype)


def _relu2_matmul(a, w, *, bm=1024, bn=1024):
    m, k = a.shape
    n = w.shape[1]
    return pl.pallas_call(
        _relu2_mm_kernel,
        grid=(m // bm, n // bn),
        in_specs=[pl.BlockSpec((bm, k), lambda i, j: (i, 0)),
                  pl.BlockSpec((k, bn), lambda i, j: (0, j))],
        out_specs=pl.BlockSpec((bm, bn), lambda i, j: (i, j)),
        out_shape=jax.ShapeDtypeStruct((m, n), BF16),
        compiler_params=_params("arbitrary", "arbitrary"),
        name="mlp_up",
    )(a, w)


def _residual_mm_ksplit_kernel(a_ref, w_ref, x_ref, g_ref, o_ref, acc_ref):
    kk = pl.program_id(2)

    @pl.when(kk == 0)
    def _():
        acc_ref[...] = jnp.zeros_like(acc_ref)

    acc_ref[...] += jnp.dot(a_ref[...], w_ref[...], preferred_element_type=F32)

    @pl.when(kk == pl.num_programs(2) - 1)
    def _():
        o_ref[...] = x_ref[...] + g_ref[...] * acc_ref[...]


def _residual_matmul_ksplit(a, w, x2d, gate, seq, *, bm=1024, bn=1024, bk=2048):
    m, k = a.shape
    n = w.shape[1]
    bsz = gate.shape[0]
    per_seq = seq // bm
    return pl.pallas_call(
        _residual_mm_ksplit_kernel,
        grid=(m // bm, n // bn, k // bk),
        in_specs=[pl.BlockSpec((bm, bk), lambda i, j, kk: (i, kk)),
                  pl.BlockSpec((bk, bn), lambda i, j, kk: (kk, j)),
                  pl.BlockSpec((bm, bn), lambda i, j, kk: (i, j)),
                  pl.BlockSpec((None, 1, bn), lambda i, j, kk: (i // per_seq, 0, j))],
        out_specs=pl.BlockSpec((bm, bn), lambda i, j, kk: (i, j)),
        out_shape=jax.ShapeDtypeStruct((m, n), F32),
        scratch_shapes=[pltpu.VMEM((bm, bn), F32)],
        compiler_params=_params("arbitrary", "arbitrary", "arbitrary"),
        name="mlp_down",
    )(a, w, x2d, gate.reshape(bsz, 1, n))


def _suffix_sum_matrix():
    r = np.arange(2 * LANES)[:, None] % LANES
    c = np.arange(2 * LANES)[None, :]
    return jnp.asarray(np.where(c < LANES, r > c, True), dtype=BF16)


def _sb_kernel(q_ref, k_ref, v_ref, ut_ref, o_ref, acc_ref, carry_ref, *, tq, tk, scale):
    seq = q_ref.shape[0]
    n_sub = tk // LANES
    n_heads = q_ref.shape[1] // HEAD_DIM
    n_diag = tq // tk
    z_scale = scale * LOG2_E

    def visit(hd, q0, k0, diag_offset):
        hcols = slice(hd * HEAD_DIM, (hd + 1) * HEAD_DIM)
        q = q_ref[pl.ds(q0, tq), hcols]
        k = k_ref[pl.ds(k0, tk), hcols]
        v = v_ref[pl.ds(k0, tk), hcols]
        z2 = lax.dot_general(q, k, (((1,), (1,)), ((), ())), preferred_element_type=F32) * z_scale
        nlr = jnp.maximum(z2, 0.0) + jnp.log2(1.0 + jnp.exp2(-jnp.abs(z2)))
        log_beta = z2 - nlr
        if diag_offset is not None:
            row = lax.broadcasted_iota(jnp.int32, (tq, tk), 0)
            col = lax.broadcasted_iota(jnp.int32, (tq, tk), 1)
            causal = col + diag_offset < row
            nlr = jnp.where(causal, nlr, 0.0)
        carry = carry_ref[hd]
        parts = [None] * n_sub
        for sb in reversed(range(n_sub)):
            cols = slice(sb * LANES, (sb + 1) * LANES)
            x = nlr[:, cols]
            hi = x.astype(BF16)
            lo = (x - hi.astype(F32)).astype(BF16)
            cs = jnp.dot(jnp.concatenate([hi, lo], axis=1), ut_ref[...],
                         preferred_element_type=F32)
            a = jnp.exp2(log_beta[:, cols] - cs[:, :LANES] - carry)
            if diag_offset is not None:
                a = jnp.where(causal[:, cols], a, 0.0)
            parts[sb] = a.astype(BF16)
            carry = carry + cs[:, LANES:]
        carry_ref[hd] = carry
        acc_ref[hd] += jnp.dot(jnp.concatenate(parts, axis=1), v, preferred_element_type=F32)

    def q_block(iq, _):
        q0 = pl.multiple_of(iq * tq, tq)
        acc_ref[...] = jnp.zeros_like(acc_ref)
        carry_ref[...] = jnp.zeros_like(carry_ref)
        for d in reversed(range(n_diag)):
            for hd in range(n_heads):
                visit(hd, q0, q0 + d * tk, d * tk)

        def left(t, _):
            k0 = pl.multiple_of(q0 - (t + 1) * tk, tk)
            for hd in range(n_heads):
                visit(hd, q0, k0, None)
            return 0

        lax.fori_loop(0, iq * n_diag, left, 0)
        for hd in range(n_heads):
            o_ref[pl.ds(q0, tq), hd * HEAD_DIM:(hd + 1) * HEAD_DIM] = acc_ref[hd].astype(o_ref.dtype)
        return 0

    lax.fori_loop(0, seq // tq, q_block, 0)


def _stick_breaking(proj, bsz, seq, n_heads, q_col, k_col, v_col, *, tq=512, tk=256, heads_per_step=4):
    hps = min(heads_per_step, n_heads)
    wb = hps * HEAD_DIM
    qb, kb, vb = q_col // wb, k_col // wb, v_col // wb
    kern = functools.partial(_sb_kernel, tq=tq, tk=tk, scale=HEAD_DIM ** -0.5)
    return pl.pallas_call(
        kern,
        grid=(bsz, n_heads // hps),
        in_specs=[pl.BlockSpec((seq, wb), lambda b, h: (b, qb + h)),
                  pl.BlockSpec((seq, wb), lambda b, h: (b, kb + h)),
                  pl.BlockSpec((seq, wb), lambda b, h: (b, vb + h)),
                  pl.BlockSpec((2 * LANES, 2 * LANES), lambda b, h: (0, 0))],
        out_specs=pl.BlockSpec((seq, wb), lambda b, h: (b, h)),
        out_shape=jax.ShapeDtypeStruct((bsz * seq, n_heads * HEAD_DIM), BF16),
        scratch_shapes=[pltpu.VMEM((hps, tq, HEAD_DIM), F32), pltpu.VMEM((hps, tq, LANES), F32)],
        compiler_params=_params("arbitrary", "arbitrary"),
        name="stick_breaking",
    )(proj, proj, proj, _suffix_sum_matrix())


DIL_BLOCK = 128


def _dil_heads(q_ref, kp_ref, kc_ref, vp_ref, vc_ref, first, n_heads, scale):
    blk = DIL_BLOCK
    row = lax.broadcasted_iota(jnp.int32, (blk, 2 * blk), 0)
    col = lax.broadcasted_iota(jnp.int32, (blk, 2 * blk), 1)
    valid = jnp.logical_and(col >= row, col <= row + blk)
    valid = jnp.logical_and(valid, jnp.logical_or(col >= blk, jnp.logical_not(first)))
    for h in range(n_heads):
        cols = slice(h * HEAD_DIM, (h + 1) * HEAD_DIM)
        q = q_ref[:, cols]
        k = jnp.concatenate([kp_ref[:, cols], kc_ref[:, cols]], axis=0)
        v = jnp.concatenate([vp_ref[:, cols], vc_ref[:, cols]], axis=0)
        s = lax.dot_general(q, k, (((1,), (1,)), ((), ())), preferred_element_type=F32) * scale
        s = jnp.where(valid, s, -jnp.inf)
        m = jnp.max(s, axis=1, keepdims=True)
        p = jnp.exp(s - m)
        l = jnp.sum(p, axis=1, keepdims=True)
        o = jnp.dot(p.astype(BF16), v, preferred_element_type=F32) / l
        yield h, cols, o, m + jnp.log(l)


def _dil_group_kernel(q_ref, kp_ref, kc_ref, vp_ref, vc_ref, o_ref, lse_ref, *, n_heads, scale):
    first = pl.program_id(2) == 0
    lane = lax.broadcasted_iota(jnp.int32, lse_ref.shape, 1)
    lse_blk = jnp.zeros(lse_ref.shape, F32)
    for h, cols, o, lse in _dil_heads(q_ref, kp_ref, kc_ref, vp_ref, vc_ref, first, n_heads, scale):
        o_ref[:, cols] = o
        lse_blk = jnp.where(lane == h, lse, lse_blk)
    lse_ref[...] = lse_blk


def _dil_combine_kernel(q_ref, kp_ref, kc_ref, vp_ref, vc_ref, o2_ref, l2_ref, o3_ref, l3_ref,
                        y_ref, *, n_heads, scale):
    first = pl.program_id(1) == 0
    for h, cols, o1, lse1 in _dil_heads(q_ref, kp_ref, kc_ref, vp_ref, vc_ref, first, n_heads, scale):
        lse2 = l2_ref[:, h:h + 1]
        lse3 = l3_ref[:, h:h + 1]
        mx = jnp.maximum(jnp.maximum(lse1, lse2), lse3)
        w1, w2, w3 = jnp.exp(lse1 - mx), jnp.exp(lse2 - mx), jnp.exp(lse3 - mx)
        y = (w1 * o1 + w2 * o2_ref[:, cols] + w3 * o3_ref[:, cols]) / (w1 + w2 + w3)
        y_ref[:, cols] = y.astype(y_ref.dtype)


def _dilated_attention(proj, bsz, seq, n_heads, q_col, k_col, v_col):
    n, d_in = proj.shape
    gw = n_heads * HEAD_DIM
    blk = DIL_BLOCK
    scale = HEAD_DIM ** -0.5
    partial = []
    for g in range(N_DIL - 1, 0, -1):
        dil = DIL_GROUPS[g][1]
        sub = seq // dil
        nblk = sub // blk
        view = proj.reshape(n // dil, dil * d_in)
        per_c = d_in // gw
        qo, ko, vo = (q_col // gw + g), (k_col // gw + g), (v_col // gw + g)

        def cur(off):
            return lambda b, c, a: (b * nblk + a, c * per_c + off)

        def prev(off):
            return lambda b, c, a: (b * nblk + jnp.maximum(a - 1, 0), c * per_c + off)

        o_g, lse_g = pl.pallas_call(
            functools.partial(_dil_group_kernel, n_heads=n_heads, scale=scale),
            grid=(bsz, dil, nblk),
            in_specs=[pl.BlockSpec((blk, gw), cur(qo)),
                      pl.BlockSpec((blk, gw), prev(ko)),
                      pl.BlockSpec((blk, gw), cur(ko)),
                      pl.BlockSpec((blk, gw), prev(vo)),
                      pl.BlockSpec((blk, gw), cur(vo))],
            out_specs=[pl.BlockSpec((blk, gw), lambda b, c, a: (b * nblk + a, c)),
                       pl.BlockSpec((blk, LANES), lambda b, c, a: (b * nblk + a, c))],
            out_shape=[jax.ShapeDtypeStruct((n // dil, dil * gw), F32),
                       jax.ShapeDtypeStruct((n // dil, dil * LANES), F32)],
            compiler_params=_params("arbitrary", "arbitrary", "arbitrary"),
            name=f"dilated_group{g}",
        )(view, view, view, view, view)
        partial.append((o_g.reshape(n, gw), lse_g.reshape(n, LANES)))

    (o3, l3), (o2, l2) = partial
    nblk = seq // blk
    qo, ko, vo = q_col // gw, k_col // gw, v_col // gw

    def cur(off):
        return lambda b, a: (b * nblk + a, off)

    def prev(off):
        return lambda b, a: (b * nblk + jnp.maximum(a - 1, 0), off)

    same = lambda b, a: (b * nblk + a, 0)
    return pl.pallas_call(
        functools.partial(_dil_combine_kernel, n_heads=n_heads, scale=scale),
        grid=(bsz, nblk),
        in_specs=[pl.BlockSpec((blk, gw), cur(qo)),
                  pl.BlockSpec((blk, gw), prev(ko)),
                  pl.BlockSpec((blk, gw), cur(ko)),
                  pl.BlockSpec((blk, gw), prev(vo)),
                  pl.BlockSpec((blk, gw), cur(vo)),
                  pl.BlockSpec((blk, gw), same),
                  pl.BlockSpec((blk, LANES), same),
                  pl.BlockSpec((blk, gw), same),
                  pl.BlockSpec((blk, LANES), same)],
        out_specs=pl.BlockSpec((blk, gw), same),
        out_shape=jax.ShapeDtypeStruct((n, gw), BF16),
        compiler_params=_params("arbitrary", "arbitrary"),
        name="dilated_group0_combine",
    )(proj, proj, proj, proj, proj, o2, l2, o3, l3)


def kernel(x, c, positions, ada_w, ada_b, norm_mix_w, w_in, w_o_sb, w_o_dil, w_out,
           norm_mlp_w, w_ff1, w_ff2, norm_final_w):
    bsz, seq, d = x.shape
    n = bsz * seq
    d_sb = w_o_sb.shape[1]
    d_dil = w_o_dil.shape[1]
    sb_heads = d_sb // HEAD_DIM
    dil_heads = d_dil // HEAD_DIM
    q_sb, k_sb, v_sb = 0, d_sb, 2 * d_sb
    q_dil = 3 * d_sb
    k_dil = q_dil + N_DIL * d_dil
    v_dil = k_dil + N_DIL * d_dil
    g_sb = v_dil + N_DIL * d_dil
    g_dil = g_sb + d

    cos, sin = _rope_tables(positions)
    xf = x.reshape(n, d)
    for l in range(ada_w.shape[0]):
        mod = _ada_modulation(c, ada_w[l], ada_b[l])
        sh1, sc1, g1, sh2, sc2, g2 = [mod[:, i * d:(i + 1) * d] for i in range(6)]

        h = _norm_mod(xf, norm_mix_w[l], sc1, sh1, seq)
        proj = _in_projection(h, w_in[l].astype(BF16), cos, sin, (q_dil, v_dil))
        y_a = _stick_breaking(proj, bsz, seq, sb_heads, q_sb, k_sb, v_sb)
        y_b = _dilated_attention(proj, bsz, seq, dil_heads, q_dil, k_dil, v_dil)
        merged = _gated_merge(y_a, w_o_sb[l].astype(BF16), y_b, w_o_dil[l].astype(BF16),
                              proj, g_sb, g_dil)
        xf = _residual_matmul(merged, w_out[l].astype(BF16), xf, g1, seq)

        h = _norm_mod(xf, norm_mlp_w[l], sc2, sh2, seq)
        u = _relu2_matmul(h, w_ff1[l].astype(BF16))
        xf = _residual_matmul_ksplit(u, w_ff2[l].astype(BF16), xf, g2, seq)
    return _final_norm(xf, norm_final_w).reshape(bsz, seq, d)
```

```python
import functools

import jax
import jax.numpy as jnp
import numpy as np
from jax import lax
from jax.experimental import pallas as pl
from jax.experimental.pallas import tpu as pltpu

F32 = jnp.float32
BF16 = jnp.bfloat16

HEAD_DIM = 128
LANES = 128
DIL_GROUPS = ((128, 1), (512, 4), (2048, 16))
N_DIL = len(DIL_GROUPS)
ROPE_THETA = 10000.0
EPS = 1e-6
LOG2_E = 1.4426950408889634
VMEM_LIMIT = 56 * 1024 * 1024


def _params(*sem):
    return pltpu.CompilerParams(dimension_semantics=sem, vmem_limit_bytes=VMEM_LIMIT)


def _ada_kernel(c_ref, w_ref, b_ref, o_ref):
    c = c_ref[...]
    s = c * (1.0 / (1.0 + jnp.exp(-c)))
    o_ref[...] = jnp.dot(s.astype(BF16), w_ref[...].astype(BF16),
                         preferred_element_type=F32) + b_ref[...]


def _ada_modulation(c, w, b, *, bn=512):
    bsz, d = c.shape
    n = w.shape[1]
    rows = 8
    c_pad = jnp.zeros((rows, d), F32).at[:bsz].set(c)
    out = pl.pallas_call(
        _ada_kernel,
        grid=(n // bn,),
        in_specs=[pl.BlockSpec((rows, d), lambda j: (0, 0)),
                  pl.BlockSpec((d, bn), lambda j: (0, j)),
                  pl.BlockSpec((1, bn), lambda j: (0, j))],
        out_specs=pl.BlockSpec((rows, bn), lambda j: (0, j)),
        out_shape=jax.ShapeDtypeStruct((rows, n), F32),
        compiler_params=_params("arbitrary"),
        name="ada_modulation",
    )(c_pad, w, b.reshape(1, n))
    return out[:bsz]


def _norm_mod_kernel(x_ref, w_ref, sc_ref, sh_ref, o_ref):
    x = x_ref[...]
    inv = lax.rsqrt(jnp.mean(x * x, axis=-1, keepdims=True) + EPS)
    mul = w_ref[...] * (1.0 + sc_ref[...])
    o_ref[...] = (x * inv * mul + sh_ref[...]).astype(o_ref.dtype)


def _norm_mod(x2d, w, sc, sh, seq, *, tm=256):
    n, d = x2d.shape
    bsz = sc.shape[0]
    per_seq = seq // tm
    return pl.pallas_call(
        _norm_mod_kernel,
        grid=(n // tm,),
        in_specs=[pl.BlockSpec((tm, d), lambda i: (i, 0)),
                  pl.BlockSpec((1, d), lambda i: (0, 0)),
                  pl.BlockSpec((None, 1, d), lambda i: (i // per_seq, 0, 0)),
                  pl.BlockSpec((None, 1, d), lambda i: (i // per_seq, 0, 0))],
        out_specs=pl.BlockSpec((tm, d), lambda i: (i, 0)),
        out_shape=jax.ShapeDtypeStruct((n, d), BF16),
        compiler_params=_params("arbitrary"),
        name="norm_modulate",
    )(x2d, w.reshape(1, d), sc.reshape(bsz, 1, d), sh.reshape(bsz, 1, d))


def _final_norm_kernel(x_ref, w_ref, o_ref):
    x = x_ref[...]
    inv = lax.rsqrt(jnp.mean(x * x, axis=-1, keepdims=True) + EPS)
    o_ref[...] = x * inv * w_ref[...]


def _final_norm(x2d, w, *, tm=256):
    n, d = x2d.shape
    return pl.pallas_call(
        _final_norm_kernel,
        grid=(n // tm,),
        in_specs=[pl.BlockSpec((tm, d), lambda i: (i, 0)),
                  pl.BlockSpec((1, d), lambda i: (0, 0))],
        out_specs=pl.BlockSpec((tm, d), lambda i: (i, 0)),
        out_shape=jax.ShapeDtypeStruct((n, d), F32),
        compiler_params=_params("arbitrary"),
        name="final_norm",
    )(x2d, w.reshape(1, d))


def _rope_table_kernel(pos_ref, invf_ref, sign_ref, cos_ref, sin_ref):
    ang = pos_ref[...].astype(F32) * invf_ref[...]
    cos_ref[...] = jnp.cos(ang)
    sin_ref[...] = jnp.sin(ang) * sign_ref[...]


def _rope_tables(positions, *, tm=2048):
    n = positions.size
    half = HEAD_DIM // 2
    inv_freq = ROPE_THETA ** (-jnp.arange(half, dtype=F32) / half)
    invf = jnp.concatenate([inv_freq, inv_freq]).reshape(1, HEAD_DIM)
    sign = jnp.concatenate([-jnp.ones((half,), F32), jnp.ones((half,), F32)]).reshape(1, HEAD_DIM)
    tm = min(tm, n)
    return pl.pallas_call(
        _rope_table_kernel,
        grid=(n // tm,),
        in_specs=[pl.BlockSpec((tm, 1), lambda i: (i, 0)),
                  pl.BlockSpec((1, HEAD_DIM), lambda i: (0, 0)),
                  pl.BlockSpec((1, HEAD_DIM), lambda i: (0, 0))],
        out_specs=[pl.BlockSpec((tm, HEAD_DIM), lambda i: (i, 0)),
                   pl.BlockSpec((tm, HEAD_DIM), lambda i: (i, 0))],
        out_shape=[jax.ShapeDtypeStruct((n, HEAD_DIM), F32),
                   jax.ShapeDtypeStruct((n, HEAD_DIM), F32)],
        compiler_params=_params("arbitrary"),
        name="rope_tables",
    )(positions.reshape(n, 1), invf, sign)


def _inproj_kernel(tile_ref, rope_ref, a_ref, w_ref, cos_ref, sin_ref, o_ref, *, head_major):
    del tile_ref
    j = pl.program_id(1)
    acc = jnp.dot(a_ref[...], w_ref[...], preferred_element_type=F32)
    n_heads = acc.shape[1] // HEAD_DIM

    def write(h, val):
        if head_major:
            o_ref[h] = val.astype(o_ref.dtype)
        else:
            o_ref[:, h * HEAD_DIM:(h + 1) * HEAD_DIM] = val.astype(o_ref.dtype)

    @pl.when(rope_ref[j] == 1)
    def _():
        cos = cos_ref[...]
        sin = sin_ref[...]
        for h in range(n_heads):
            xh = acc[:, h * HEAD_DIM:(h + 1) * HEAD_DIM]
            write(h, xh * cos + pltpu.roll(xh, HEAD_DIM // 2, 1) * sin)

    @pl.when(rope_ref[j] == 0)
    def _():
        for h in range(n_heads):
            write(h, acc[:, h * HEAD_DIM:(h + 1) * HEAD_DIM])


def _in_projection(h, w, cos, sin, col_tiles, rope_flags, *, head_major, bm=1024, bn=1024):
    m, k = h.shape
    nt = len(col_tiles)
    if head_major:
        out_spec = pl.BlockSpec((bn // HEAD_DIM, bm, HEAD_DIM), lambda i, j, t, r: (j, i, 0))
        out_shape = jax.ShapeDtypeStruct((nt * (bn // HEAD_DIM), m, HEAD_DIM), F32)
    else:
        out_spec = pl.BlockSpec((bm, bn), lambda i, j, t, r: (i, j))
        out_shape = jax.ShapeDtypeStruct((m, nt * bn), BF16)
    return pl.pallas_call(
        functools.partial(_inproj_kernel, head_major=head_major),
        grid_spec=pltpu.PrefetchScalarGridSpec(
            num_scalar_prefetch=2,
            grid=(m // bm, nt),
            in_specs=[pl.BlockSpec((bm, k), lambda i, j, t, r: (i, 0)),
                      pl.BlockSpec((k, bn), lambda i, j, t, r: (0, t[j])),
                      pl.BlockSpec((bm, HEAD_DIM), lambda i, j, t, r: (i, 0)),
                      pl.BlockSpec((bm, HEAD_DIM), lambda i, j, t, r: (i, 0))],
            out_specs=out_spec),
        out_shape=out_shape,
        compiler_params=_params("arbitrary", "arbitrary"),
        name="in_projection_heads" if head_major else "in_projection",
    )(jnp.asarray(col_tiles, jnp.int32), jnp.asarray(rope_flags, jnp.int32), h, w, cos, sin)


def _merge_kernel(ya_ref, wa_ref, yb_ref, wb_ref, ga_ref, gb_ref, o_ref):
    a = jnp.dot(ya_ref[...], wa_ref[...], preferred_element_type=F32)
    b = jnp.dot(yb_ref[...], wb_ref[...], preferred_element_type=F32)
    ga = ga_ref[...].astype(F32)
    gb = gb_ref[...].astype(F32)
    o_ref[...] = (a / (1.0 + jnp.exp(-ga)) + b / (1.0 + jnp.exp(-gb))).astype(o_ref.dtype)


def _gated_merge(ya, wa, yb, wb, proj, ga_col, gb_col, *, bm=1024, bn=1024):
    m = ya.shape[0]
    n = wa.shape[1]
    ga_blk, gb_blk = ga_col // bn, gb_col // bn
    return pl.pallas_call(
        _merge_kernel,
        grid=(m // bm, n // bn),
        in_specs=[pl.BlockSpec((bm, ya.shape[1]), lambda i, j: (i, 0)),
                  pl.BlockSpec((wa.shape[0], bn), lambda i, j: (0, j)),
                  pl.BlockSpec((bm, yb.shape[1]), lambda i, j: (i, 0)),
                  pl.BlockSpec((wb.shape[0], bn), lambda i, j: (0, j)),
                  pl.BlockSpec((bm, bn), lambda i, j: (i, ga_blk + j)),
                  pl.BlockSpec((bm, bn), lambda i, j: (i, gb_blk + j))],
        out_specs=pl.BlockSpec((bm, bn), lambda i, j: (i, j)),
        out_shape=jax.ShapeDtypeStruct((m, n), BF16),
        compiler_params=_params("arbitrary", "arbitrary"),
        name="gated_merge",
    )(ya, wa, yb, wb, proj, proj)


def _residual_mm_kernel(a_ref, w_ref, x_ref, g_ref, o_ref):
    acc = jnp.dot(a_ref[...], w_ref[...], preferred_element_type=F32)
    o_ref[...] = x_ref[...] + g_ref[...] * acc


def _residual_matmul(a, w, x2d, gate, seq, *, bm=1024, bn=1024):
    m, k = a.shape
    n = w.shape[1]
    bsz = gate.shape[0]
    per_seq = seq // bm
    return pl.pallas_call(
        _residual_mm_kernel,
        grid=(m // bm, n // bn),
        in_specs=[pl.BlockSpec((bm, k), lambda i, j: (i, 0)),
                  pl.BlockSpec((k, bn), lambda i, j: (0, j)),
                  pl.BlockSpec((bm, bn), lambda i, j: (i, j)),
                  pl.BlockSpec((None, 1, bn), lambda i, j: (i // per_seq, 0, j))],
        out_specs=pl.BlockSpec((bm, bn), lambda i, j: (i, j)),
        out_shape=jax.ShapeDtypeStruct((m, n), F32),
        compiler_params=_params("arbitrary", "arbitrary"),
        name="out_projection",
    )(a, w, x2d, gate.reshape(bsz, 1, n))


def _relu2_mm_kernel(a_ref, w_ref, o_ref):
    acc = jnp.dot(a_ref[...], w_ref[...], preferred_element_type=F32)
    r = jnp.maximum(acc, 0.0)
    o_ref[...] = (r * r).astype(o_ref.dtype)


def _relu2_matmul(a, w, *, bm=1024, bn=1024):
    m, k = a.shape
    n = w.shape[1]
    return pl.pallas_call(
        _relu2_mm_kernel,
        grid=(m // bm, n // bn),
        in_specs=[pl.BlockSpec((bm, k), lambda i, j: (i, 0)),
                  pl.BlockSpec((k, bn), lambda i, j: (0, j))],
        out_specs=pl.BlockSpec((bm, bn), lambda i, j: (i, j)),
        out_shape=jax.ShapeDtypeStruct((m, n), BF16),
        compiler_params=_params("arbitrary", "arbitrary"),
        name="mlp_up",
    )(a, w)


def _residual_mm_ksplit_kernel(a_ref, w_ref, x_ref, g_ref, o_ref, acc_ref):
    kk = pl.program_id(2)

    @pl.when(kk == 0)
    def _():
        acc_ref[...] = jnp.zeros_like(acc_ref)

    acc_ref[...] += jnp.dot(a_ref[...], w_ref[...], preferred_element_type=F32)

    @pl.when(kk == pl.num_programs(2) - 1)
    def _():
        o_ref[...] = x_ref[...] + g_ref[...] * acc_ref[...]


def _residual_matmul_ksplit(a, w, x2d, gate, seq, *, bm=1024, bn=1024, bk=2048):
    m, k = a.shape
    n = w.shape[1]
    bsz = gate.shape[0]
    per_seq = seq // bm
    return pl.pallas_call(
        _residual_mm_ksplit_kernel,
        grid=(m // bm, n // bn, k // bk),
        in_specs=[pl.BlockSpec((bm, bk), lambda i, j, kk: (i, kk)),
                  pl.BlockSpec((bk, bn), lambda i, j, kk: (kk, j)),
                  pl.BlockSpec((bm, bn), lambda i, j, kk: (i, j)),
                  pl.BlockSpec((None, 1, bn), lambda i, j, kk: (i // per_seq, 0, j))],
        out_specs=pl.BlockSpec((bm, bn), lambda i, j, kk: (i, j)),
        out_shape=jax.ShapeDtypeStruct((m, n), F32),
        scratch_shapes=[pltpu.VMEM((bm, bn), F32)],
        compiler_params=_params("arbitrary", "arbitrary", "arbitrary"),
        name="mlp_down",
    )(a, w, x2d, gate.reshape(bsz, 1, n))


def _suffix_sum_matrix():
    r = np.arange(2 * LANES)[:, None] % LANES
    c = np.arange(2 * LANES)[None, :]
    return jnp.asarray(np.where(c < LANES, r > c, True), dtype=BF16)


def _sb_kernel(q_ref, k_ref, v_ref, ut_ref, o_ref, acc_ref, carry_ref, *, tq, tk, scale):
    seq = q_ref.shape[0]
    n_sub = tk // LANES
    n_heads = q_ref.shape[1] // HEAD_DIM
    n_diag = tq // tk
    z_scale = scale * LOG2_E

    def visit(hd, q0, k0, diag_offset):
        hcols = slice(hd * HEAD_DIM, (hd + 1) * HEAD_DIM)
        q = q_ref[pl.ds(q0, tq), hcols]
        k = k_ref[pl.ds(k0, tk), hcols]
        v = v_ref[pl.ds(k0, tk), hcols]
        z2 = lax.dot_general(q, k, (((1,), (1,)), ((), ())), preferred_element_type=F32) * z_scale
        nlr = jnp.maximum(z2, 0.0) + jnp.log2(1.0 + jnp.exp2(-jnp.abs(z2)))
        log_beta = z2 - nlr
        if diag_offset is not None:
            row = lax.broadcasted_iota(jnp.int32, (tq, tk), 0)
            col = lax.broadcasted_iota(jnp.int32, (tq, tk), 1)
            causal = col + diag_offset < row
            nlr = jnp.where(causal, nlr, 0.0)
        carry = carry_ref[hd]
        parts = [None] * n_sub
        for sb in reversed(range(n_sub)):
            cols = slice(sb * LANES, (sb + 1) * LANES)
            x = nlr[:, cols]
            hi = x.astype(BF16)
            lo = (x - hi.astype(F32)).astype(BF16)
            cs = jnp.dot(jnp.concatenate([hi, lo], axis=1), ut_ref[...],
                         preferred_element_type=F32)
            a = jnp.exp2(log_beta[:, cols] - cs[:, :LANES] - carry)
            if diag_offset is not None:
                a = jnp.where(causal[:, cols], a, 0.0)
            parts[sb] = a.astype(BF16)
            carry = carry + cs[:, LANES:]
        carry_ref[hd] = carry
        acc_ref[hd] += jnp.dot(jnp.concatenate(parts, axis=1), v, preferred_element_type=F32)

    def q_block(iq, _):
        q0 = pl.multiple_of(iq * tq, tq)
        acc_ref[...] = jnp.zeros_like(acc_ref)
        carry_ref[...] = jnp.zeros_like(carry_ref)
        for d in reversed(range(n_diag)):
            for hd in range(n_heads):
                visit(hd, q0, q0 + d * tk, d * tk)

        def left(t, _):
            k0 = pl.multiple_of(q0 - (t + 1) * tk, tk)
            for hd in range(n_heads):
                visit(hd, q0, k0, None)
            return 0

        lax.fori_loop(0, iq * n_diag, left, 0)
        for hd in range(n_heads):
            o_ref[pl.ds(q0, tq), hd * HEAD_DIM:(hd + 1) * HEAD_DIM] = acc_ref[hd].astype(o_ref.dtype)
        return 0

    lax.fori_loop(0, seq // tq, q_block, 0)


def _stick_breaking(proj, bsz, seq, n_heads, q_col, k_col, v_col, *, tq=512, tk=256, heads_per_step=4):
    hps = min(heads_per_step, n_heads)
    wb = hps * HEAD_DIM
    qb, kb, vb = q_col // wb, k_col // wb, v_col // wb
    kern = functools.partial(_sb_kernel, tq=tq, tk=tk, scale=HEAD_DIM ** -0.5)
    return pl.pallas_call(
        kern,
        grid=(bsz, n_heads // hps),
        in_specs=[pl.BlockSpec((seq, wb), lambda b, h: (b, qb + h)),
                  pl.BlockSpec((seq, wb), lambda b, h: (b, kb + h)),
                  pl.BlockSpec((seq, wb), lambda b, h: (b, vb + h)),
                  pl.BlockSpec((2 * LANES, 2 * LANES), lambda b, h: (0, 0))],
        out_specs=pl.BlockSpec((seq, wb), lambda b, h: (b, h)),
        out_shape=jax.ShapeDtypeStruct((bsz * seq, n_heads * HEAD_DIM), BF16),
        scratch_shapes=[pltpu.VMEM((hps, tq, HEAD_DIM), F32), pltpu.VMEM((hps, tq, LANES), F32)],
        compiler_params=_params("arbitrary", "arbitrary"),
        name="stick_breaking",
    )(proj, proj, proj, _suffix_sum_matrix())


WINDOW = 128
DIL_TILE = 1024


def _window_softmax(q, k, v, valid, scale):
    s = lax.dot_general(q, k, (((1,), (1,)), ((), ())), preferred_element_type=F32) * scale
    s = jnp.where(valid, s, -jnp.inf)
    m = jnp.max(s, axis=1, keepdims=True)
    p = jnp.exp(s - m)
    l = jnp.sum(p, axis=1, keepdims=True)
    o = jnp.dot(p.astype(BF16), v, preferred_element_type=F32) / l
    return o, m + jnp.log(l)


def _dil_strided_kernel(q_ref, k_ref, v_ref, o_ref, lse_ref, kd_ref, vd_ref, *, dil, scale):
    n_heads, tile, _ = q_ref.shape
    chunk = tile // dil
    n_hist = -(-WINDOW // chunk)
    n_slots = n_hist + 1
    takes = [min(chunk, WINDOW - (back - 1) * chunk) for back in range(1, n_hist + 1)]
    n_keys = WINDOW + chunk
    kt = pl.program_id(1)
    slot = lax.rem(kt, n_slots)

    @pl.when(kt == 0)
    def _():
        kd_ref[...] = jnp.zeros_like(kd_ref)
        vd_ref[...] = jnp.zeros_like(vd_ref)

    def deinterleave(c, _):
        rows = pl.ds(pl.multiple_of(c * chunk, chunk), chunk)
        for h in range(n_heads):
            kd_ref[slot, h, rows, :] = k_ref[h, pl.ds(c, chunk, stride=dil), :].astype(BF16)
            vd_ref[slot, h, rows, :] = v_ref[h, pl.ds(c, chunk, stride=dil), :].astype(BF16)
        return 0

    lax.fori_loop(0, dil, deinterleave, 0)

    row = lax.broadcasted_iota(jnp.int32, (chunk, n_keys), 0)
    col = lax.broadcasted_iota(jnp.int32, (chunk, n_keys), 1)
    valid = jnp.logical_and(col >= row, col <= row + WINDOW)
    valid = jnp.logical_and(valid, col >= WINDOW - chunk * kt)
    lane = lax.broadcasted_iota(jnp.int32, (chunk, LANES), 1)

    def attend(c, _):
        base = pl.multiple_of(c * chunk, chunk)
        lse_blk = jnp.zeros((chunk, LANES), F32)
        for h in range(n_heads):
            q = q_ref[h, pl.ds(c, chunk, stride=dil), :].astype(BF16)
            ks, vs = [], []
            for back in range(n_hist, 0, -1):
                take = takes[back - 1]
                old = lax.rem(kt + (n_slots - back), n_slots)
                rows = pl.ds(pl.multiple_of(base + (chunk - take), 16), take)
                ks.append(kd_ref[old, h, rows, :])
                vs.append(vd_ref[old, h, rows, :])
            ks.append(kd_ref[slot, h, pl.ds(base, chunk), :])
            vs.append(vd_ref[slot, h, pl.ds(base, chunk), :])
            o, lse = _window_softmax(q, jnp.concatenate(ks, axis=0), jnp.concatenate(vs, axis=0),
                                     valid, scale)
            o_ref[h, pl.ds(c, chunk, stride=dil), :] = o
            lse_blk = jnp.where(lane == h, lse, lse_blk)
        lse_ref[pl.ds(c, chunk, stride=dil), :] = lse_blk
        return 0

    lax.fori_loop(0, dil, attend, 0)


def _dil_strided(heads_qkv, g, bsz, seq, n_heads):
    n = heads_qkv.shape[1]
    dil = DIL_GROUPS[g][1]
    tile = min(DIL_TILE, seq)
    chunk = tile // dil
    n_slots = -(-WINDOW // chunk) + 1
    per_seq = seq // tile

    def operand(which):
        return pl.BlockSpec((n_heads, tile, HEAD_DIM), lambda b, t: (2 * which + g - 1, b * per_seq + t, 0))

    return pl.pallas_call(
        functools.partial(_dil_strided_kernel, dil=dil, scale=HEAD_DIM ** -0.5),
        grid=(bsz, per_seq),
        in_specs=[operand(0), operand(1), operand(2)],
        out_specs=[pl.BlockSpec((n_heads, tile, HEAD_DIM), lambda b, t: (0, b * per_seq + t, 0)),
                   pl.BlockSpec((tile, LANES), lambda b, t: (b * per_seq + t, 0))],
        out_shape=[jax.ShapeDtypeStruct((n_heads, n, HEAD_DIM), F32),
                   jax.ShapeDtypeStruct((n, LANES), F32)],
        scratch_shapes=[pltpu.VMEM((n_slots, n_heads, tile, HEAD_DIM), BF16),
                        pltpu.VMEM((n_slots, n_heads, tile, HEAD_DIM), BF16)],
        compiler_params=_params("arbitrary", "arbitrary"),
        name=f"dilated_stride{dil}",
    )(heads_qkv, heads_qkv, heads_qkv)


def _dil_merge_kernel(q_ref, kp_ref, kc_ref, vp_ref, vc_ref, o2_ref, l2_ref, o3_ref, l3_ref, y_ref,
                      *, scale):
    tile = q_ref.shape[0]
    n_heads = q_ref.shape[1] // HEAD_DIM
    blk = WINDOW
    first = pl.program_id(1) == 0
    row = lax.broadcasted_iota(jnp.int32, (blk, 2 * blk), 0)
    col = lax.broadcasted_iota(jnp.int32, (blk, 2 * blk), 1)
    band = jnp.logical_and(col >= row, col <= row + blk)
    band_first = jnp.logical_and(band, jnp.logical_or(col >= blk, jnp.logical_not(first)))
    for sub in range(tile // blk):
        rows = slice(sub * blk, (sub + 1) * blk)
        for h in range(n_heads):
            cols = slice(h * HEAD_DIM, (h + 1) * HEAD_DIM)
            if sub == 0:
                k = jnp.concatenate([kp_ref[:, cols], kc_ref[rows, cols]], axis=0)
                v = jnp.concatenate([vp_ref[:, cols], vc_ref[rows, cols]], axis=0)
                valid = band_first
            else:
                k = kc_ref[(sub - 1) * blk:(sub + 1) * blk, cols]
                v = vc_ref[(sub - 1) * blk:(sub + 1) * blk, cols]
                valid = band
            o1, lse1 = _window_softmax(q_ref[rows, cols], k, v, valid, scale)
            lse2 = l2_ref[rows, h:h + 1]
            lse3 = l3_ref[rows, h:h + 1]
            mx = jnp.maximum(jnp.maximum(lse1, lse2), lse3)
            w1, w2, w3 = jnp.exp(lse1 - mx), jnp.exp(lse2 - mx), jnp.exp(lse3 - mx)
            y = (w1 * o1 + w2 * o2_ref[h, rows, :] + w3 * o3_ref[h, rows, :]) / (w1 + w2 + w3)
            y_ref[rows, cols] = y.astype(y_ref.dtype)


def _dil_merge(proj, q_col, k_col, v_col, o2, l2, o3, l3, bsz, seq, n_heads, *, tile=512):
    n = proj.shape[0]
    gw = n_heads * HEAD_DIM
    tile = min(tile, seq)
    per_seq = seq // tile
    sub_per_tile = tile // WINDOW
    sub_per_seq = seq // WINDOW
    qo, ko, vo = q_col // gw, k_col // gw, v_col // gw

    def cur(off):
        return pl.BlockSpec((tile, gw), lambda b, t: (b * per_seq + t, off))

    def prev(off):
        return pl.BlockSpec(
            (WINDOW, gw), lambda b, t: (b * sub_per_seq + jnp.maximum(t * sub_per_tile - 1, 0), off))

    heads = pl.BlockSpec((n_heads, tile, HEAD_DIM), lambda b, t: (0, b * per_seq + t, 0))
    lses = pl.BlockSpec((tile, LANES), lambda b, t: (b * per_seq + t, 0))
    return pl.pallas_call(
        functools.partial(_dil_merge_kernel, scale=HEAD_DIM ** -0.5),
        grid=(bsz, per_seq),
        in_specs=[cur(qo), prev(ko), cur(ko), prev(vo), cur(vo), heads, lses, heads, lses],
        out_specs=pl.BlockSpec((tile, gw), lambda b, t: (b * per_seq + t, 0)),
        out_shape=jax.ShapeDtypeStruct((n, gw), BF16),
        compiler_params=_params("arbitrary", "arbitrary"),
        name="dilated_merge",
    )(proj, proj, proj, proj, proj, o2, l2, o3, l3)


def _dilated_attention(proj, q_col, k_col, v_col, heads_qkv, bsz, seq, n_heads):
    o3, l3 = _dil_strided(heads_qkv, 2, bsz, seq, n_heads)
    o2, l2 = _dil_strided(heads_qkv, 1, bsz, seq, n_heads)
    return _dil_merge(proj, q_col, k_col, v_col, o2, l2, o3, l3, bsz, seq, n_heads)


def kernel(x, c, positions, ada_w, ada_b, norm_mix_w, w_in, w_o_sb, w_o_dil, w_out,
           norm_mlp_w, w_ff1, w_ff2, norm_final_w):
    bsz, seq, d = x.shape
    n = bsz * seq
    d_sb = w_o_sb.shape[1]
    d_dil = w_o_dil.shape[1]
    sb_heads = d_sb // HEAD_DIM
    dil_heads = d_dil // HEAD_DIM
    bn = d_dil
    sb_t = 3 * d_sb // bn
    dil_t = lambda which, g: sb_t + which * N_DIL + g
    gate_t = sb_t + 3 * N_DIL
    main_tiles = (list(range(sb_t)) + [dil_t(0, 0), dil_t(1, 0), dil_t(2, 0)]
                  + list(range(gate_t, gate_t + 2 * d // bn)))
    main_rope = [0] * sb_t + [1, 1, 0] + [0] * (2 * d // bn)
    q0_col, k0_col, v0_col = sb_t * bn, (sb_t + 1) * bn, (sb_t + 2) * bn
    g_sb_col = (sb_t + 3) * bn
    g_dil_col = g_sb_col + d
    head_tiles = [dil_t(which, g) for which in range(3) for g in (1, 2)]
    head_rope = [1, 1, 1, 1, 0, 0]

    cos, sin = _rope_tables(positions)
    xf = x.reshape(n, d)
    for l in range(ada_w.shape[0]):
        mod = _ada_modulation(c, ada_w[l], ada_b[l])
        sh1, sc1, g1, sh2, sc2, g2 = [mod[:, i * d:(i + 1) * d] for i in range(6)]

        h = _norm_mod(xf, norm_mix_w[l], sc1, sh1, seq)
        w_in_l = w_in[l].astype(BF16)
        proj = _in_projection(h, w_in_l, cos, sin, main_tiles, main_rope, head_major=False, bn=bn)
        heads_qkv = _in_projection(h, w_in_l, cos, sin, head_tiles, head_rope, head_major=True, bn=bn)
        y_a = _stick_breaking(proj, bsz, seq, sb_heads, 0, d_sb, 2 * d_sb)
        y_b = _dilated_attention(proj, q0_col, k0_col, v0_col, heads_qkv, bsz, seq, dil_heads)
        merged = _gated_merge(y_a, w_o_sb[l].astype(BF16), y_b, w_o_dil[l].astype(BF16),
                              proj, g_sb_col, g_dil_col)
        xf = _residual_matmul(merged, w_out[l].astype(BF16), xf, g1, seq)

        h = _norm_mod(xf, norm_mlp_w[l], sc2, sh2, seq)
        u = _relu2_matmul(h, w_ff1[l].astype(BF16))
        xf = _residual_matmul_ksplit(u, w_ff2[l].astype(BF16), xf, g2, seq)
    return _final_norm(xf, norm_final_w).reshape(bsz, seq, d)
```

```python
import functools

import jax
import jax.numpy as jnp
import numpy as np
from jax import lax
from jax.experimental import pallas as pl
from jax.experimental.pallas import tpu as pltpu

F32 = jnp.float32
BF16 = jnp.bfloat16

HEAD_DIM = 128
LANES = 128
DIL_GROUPS = ((128, 1), (512, 4), (2048, 16))
N_DIL = len(DIL_GROUPS)
ROPE_THETA = 10000.0
EPS = 1e-6
LOG2_E = 1.4426950408889634
VMEM_LIMIT = 56 * 1024 * 1024


def _params(*sem):
    return pltpu.CompilerParams(dimension_semantics=sem, vmem_limit_bytes=VMEM_LIMIT)


def _ada_kernel(c_ref, w_ref, b_ref, o_ref):
    c = c_ref[...]
    s = c * (1.0 / (1.0 + jnp.exp(-c)))
    o_ref[...] = jnp.dot(s.astype(BF16), w_ref[...].astype(BF16),
                         preferred_element_type=F32) + b_ref[...]


def _ada_modulation(c, w, b, *, bn=512):
    bsz, d = c.shape
    n = w.shape[1]
    rows = 8
    c_pad = jnp.zeros((rows, d), F32).at[:bsz].set(c)
    out = pl.pallas_call(
        _ada_kernel,
        grid=(n // bn,),
        in_specs=[pl.BlockSpec((rows, d), lambda j: (0, 0)),
                  pl.BlockSpec((d, bn), lambda j: (0, j)),
                  pl.BlockSpec((1, bn), lambda j: (0, j))],
        out_specs=pl.BlockSpec((rows, bn), lambda j: (0, j)),
        out_shape=jax.ShapeDtypeStruct((rows, n), F32),
        compiler_params=_params("arbitrary"),
        name="ada_modulation",
    )(c_pad, w, b.reshape(1, n))
    return out[:bsz]


def _norm_mod_kernel(x_ref, w_ref, sc_ref, sh_ref, o_ref):
    x = x_ref[...]
    inv = lax.rsqrt(jnp.mean(x * x, axis=-1, keepdims=True) + EPS)
    mul = w_ref[...] * (1.0 + sc_ref[...])
    o_ref[...] = (x * inv * mul + sh_ref[...]).astype(o_ref.dtype)


def _norm_mod(x2d, w, sc, sh, seq, *, tm=256):
    n, d = x2d.shape
    bsz = sc.shape[0]
    per_seq = seq // tm
    return pl.pallas_call(
        _norm_mod_kernel,
        grid=(n // tm,),
        in_specs=[pl.BlockSpec((tm, d), lambda i: (i, 0)),
                  pl.BlockSpec((1, d), lambda i: (0, 0)),
                  pl.BlockSpec((None, 1, d), lambda i: (i // per_seq, 0, 0)),
                  pl.BlockSpec((None, 1, d), lambda i: (i // per_seq, 0, 0))],
        out_specs=pl.BlockSpec((tm, d), lambda i: (i, 0)),
        out_shape=jax.ShapeDtypeStruct((n, d), BF16),
        compiler_params=_params("arbitrary"),
        name="norm_modulate",
    )(x2d, w.reshape(1, d), sc.reshape(bsz, 1, d), sh.reshape(bsz, 1, d))


def _final_norm_kernel(x_ref, w_ref, o_ref):
    x = x_ref[...]
    inv = lax.rsqrt(jnp.mean(x * x, axis=-1, keepdims=True) + EPS)
    o_ref[...] = x * inv * w_ref[...]


def _final_norm(x2d, w, *, tm=256):
    n, d = x2d.shape
    return pl.pallas_call(
        _final_norm_kernel,
        grid=(n // tm,),
        in_specs=[pl.BlockSpec((tm, d), lambda i: (i, 0)),
                  pl.BlockSpec((1, d), lambda i: (0, 0))],
        out_specs=pl.BlockSpec((tm, d), lambda i: (i, 0)),
        out_shape=jax.ShapeDtypeStruct((n, d), F32),
        compiler_params=_params("arbitrary"),
        name="final_norm",
    )(x2d, w.reshape(1, d))


def _rope_table_kernel(pos_ref, invf_ref, sign_ref, cos_ref, sin_ref):
    ang = pos_ref[...].astype(F32) * invf_ref[...]
    cos_ref[...] = jnp.cos(ang)
    sin_ref[...] = jnp.sin(ang) * sign_ref[...]


def _rope_tables(positions, *, tm=2048):
    n = positions.size
    half = HEAD_DIM // 2
    inv_freq = ROPE_THETA ** (-jnp.arange(half, dtype=F32) / half)
    invf = jnp.concatenate([inv_freq, inv_freq]).reshape(1, HEAD_DIM)
    sign = jnp.concatenate([-jnp.ones((half,), F32), jnp.ones((half,), F32)]).reshape(1, HEAD_DIM)
    tm = min(tm, n)
    return pl.pallas_call(
        _rope_table_kernel,
        grid=(n // tm,),
        in_specs=[pl.BlockSpec((tm, 1), lambda i: (i, 0)),
                  pl.BlockSpec((1, HEAD_DIM), lambda i: (0, 0)),
                  pl.BlockSpec((1, HEAD_DIM), lambda i: (0, 0))],
        out_specs=[pl.BlockSpec((tm, HEAD_DIM), lambda i: (i, 0)),
                   pl.BlockSpec((tm, HEAD_DIM), lambda i: (i, 0))],
        out_shape=[jax.ShapeDtypeStruct((n, HEAD_DIM), F32),
                   jax.ShapeDtypeStruct((n, HEAD_DIM), F32)],
        compiler_params=_params("arbitrary"),
        name="rope_tables",
    )(positions.reshape(n, 1), invf, sign)


def _inproj_kernel(tile_ref, rope_ref, a_ref, w_ref, cos_ref, sin_ref, o_ref, *, head_major):
    del tile_ref
    rope = rope_ref[pl.program_id(1)] == 1
    acc = jnp.dot(a_ref[...], w_ref[...], preferred_element_type=F32)
    cos = cos_ref[...]
    sin = sin_ref[...]
    for h in range(acc.shape[1] // HEAD_DIM):
        xh = acc[:, h * HEAD_DIM:(h + 1) * HEAD_DIM]
        val = jnp.where(rope, xh * cos + pltpu.roll(xh, HEAD_DIM // 2, 1) * sin, xh)
        if head_major:
            o_ref[h] = val.astype(o_ref.dtype)
        else:
            o_ref[:, h * HEAD_DIM:(h + 1) * HEAD_DIM] = val.astype(o_ref.dtype)


def _in_projection(h, w, cos, sin, col_tiles, rope_flags, *, head_major, bm=1024, bn=1024):
    m, k = h.shape
    nt = len(col_tiles)
    if head_major:
        out_spec = pl.BlockSpec((bn // HEAD_DIM, bm, HEAD_DIM), lambda i, j, t, r: (j, i, 0))
        out_shape = jax.ShapeDtypeStruct((nt * (bn // HEAD_DIM), m, HEAD_DIM), F32)
    else:
        out_spec = pl.BlockSpec((bm, bn), lambda i, j, t, r: (i, j))
        out_shape = jax.ShapeDtypeStruct((m, nt * bn), BF16)
    return pl.pallas_call(
        functools.partial(_inproj_kernel, head_major=head_major),
        grid_spec=pltpu.PrefetchScalarGridSpec(
            num_scalar_prefetch=2,
            grid=(m // bm, nt),
            in_specs=[pl.BlockSpec((bm, k), lambda i, j, t, r: (i, 0)),
                      pl.BlockSpec((k, bn), lambda i, j, t, r: (0, t[j])),
                      pl.BlockSpec((bm, HEAD_DIM), lambda i, j, t, r: (i, 0)),
                      pl.BlockSpec((bm, HEAD_DIM), lambda i, j, t, r: (i, 0))],
            out_specs=out_spec),
        out_shape=out_shape,
        compiler_params=_params("arbitrary", "arbitrary"),
        name="in_projection_heads" if head_major else "in_projection",
    )(jnp.asarray(col_tiles, jnp.int32), jnp.asarray(rope_flags, jnp.int32), h, w, cos, sin)


def _merge_kernel(ya_ref, wa_ref, yb_ref, wb_ref, ga_ref, gb_ref, o_ref):
    a = jnp.dot(ya_ref[...], wa_ref[...], preferred_element_type=F32)
    b = jnp.dot(yb_ref[...], wb_ref[...], preferred_element_type=F32)
    ga = ga_ref[...].astype(F32)
    gb = gb_ref[...].astype(F32)
    o_ref[...] = (a / (1.0 + jnp.exp(-ga)) + b / (1.0 + jnp.exp(-gb))).astype(o_ref.dtype)


def _gated_merge(ya, wa, yb, wb, proj, ga_col, gb_col, *, bm=1024, bn=1024):
    m = ya.shape[0]
    n = wa.shape[1]
    ga_blk, gb_blk = ga_col // bn, gb_col // bn
    return pl.pallas_call(
        _merge_kernel,
        grid=(m // bm, n // bn),
        in_specs=[pl.BlockSpec((bm, ya.shape[1]), lambda i, j: (i, 0)),
                  pl.BlockSpec((wa.shape[0], bn), lambda i, j: (0, j)),
                  pl.BlockSpec((bm, yb.shape[1]), lambda i, j: (i, 0)),
                  pl.BlockSpec((wb.shape[0], bn), lambda i, j: (0, j)),
                  pl.BlockSpec((bm, bn), lambda i, j: (i, ga_blk + j)),
                  pl.BlockSpec((bm, bn), lambda i, j: (i, gb_blk + j))],
        out_specs=pl.BlockSpec((bm, bn), lambda i, j: (i, j)),
        out_shape=jax.ShapeDtypeStruct((m, n), BF16),
        compiler_params=_params("arbitrary", "arbitrary"),
        name="gated_merge",
    )(ya, wa, yb, wb, proj, proj)


def _residual_mm_kernel(a_ref, w_ref, x_ref, g_ref, o_ref):
    acc = jnp.dot(a_ref[...], w_ref[...], preferred_element_type=F32)
    o_ref[...] = x_ref[...] + g_ref[...] * acc


def _residual_matmul(a, w, x2d, gate, seq, *, bm=1024, bn=1024):
    m, k = a.shape
    n = w.shape[1]
    bsz = gate.shape[0]
    per_seq = seq // bm
    return pl.pallas_call(
        _residual_mm_kernel,
        grid=(m // bm, n // bn),
        in_specs=[pl.BlockSpec((bm, k), lambda i, j: (i, 0)),
                  pl.BlockSpec((k, bn), lambda i, j: (0, j)),
                  pl.BlockSpec((bm, bn), lambda i, j: (i, j)),
                  pl.BlockSpec((None, 1, bn), lambda i, j: (i // per_seq, 0, j))],
        out_specs=pl.BlockSpec((bm, bn), lambda i, j: (i, j)),
        out_shape=jax.ShapeDtypeStruct((m, n), F32),
        compiler_params=_params("arbitrary", "arbitrary"),
        name="out_projection",
    )(a, w, x2d, gate.reshape(bsz, 1, n))


def _relu2_mm_kernel(a_ref, w_ref, o_ref):
    acc = jnp.dot(a_ref[...], w_ref[...], preferred_element_type=F32)
    r = jnp.maximum(acc, 0.0)
    o_ref[...] = (r * r).astype(o_ref.dtype)


def _relu2_matmul(a, w, *, bm=1024, bn=1024):
    m, k = a.shape
    n = w.shape[1]
    return pl.pallas_call(
        _relu2_mm_kernel,
        grid=(m // bm, n // bn),
        in_specs=[pl.BlockSpec((bm, k), lambda i, j: (i, 0)),
                  pl.BlockSpec((k, bn), lambda i, j: (0, j))],
        out_specs=pl.BlockSpec((bm, bn), lambda i, j: (i, j)),
        out_shape=jax.ShapeDtypeStruct((m, n), BF16),
        compiler_params=_params("arbitrary", "arbitrary"),
        name="mlp_up",
    )(a, w)


def _residual_mm_ksplit_kernel(a_ref, w_ref, x_ref, g_ref, o_ref):
    kk = pl.program_id(2)

    @pl.when(kk == 0)
    def _():
        o_ref[...] = jnp.zeros_like(o_ref)

    o_ref[...] += jnp.dot(a_ref[...], w_ref[...], preferred_element_type=F32)

    @pl.when(kk == pl.num_programs(2) - 1)
    def _():
        o_ref[...] = x_ref[...] + g_ref[...] * o_ref[...]


def _residual_matmul_ksplit(a, w, x2d, gate, seq, *, bm=1024, bn=1024, bk=4096):
    m, k = a.shape
    n = w.shape[1]
    bsz = gate.shape[0]
    per_seq = seq // bm
    return pl.pallas_call(
        _residual_mm_ksplit_kernel,
        grid=(m // bm, n // bn, k // bk),
        in_specs=[pl.BlockSpec((bm, bk), lambda i, j, kk: (i, kk)),
                  pl.BlockSpec((bk, bn), lambda i, j, kk: (kk, j)),
                  pl.BlockSpec((bm, bn), lambda i, j, kk: (i, j)),
                  pl.BlockSpec((None, 1, bn), lambda i, j, kk: (i // per_seq, 0, j))],
        out_specs=pl.BlockSpec((bm, bn), lambda i, j, kk: (i, j)),
        out_shape=jax.ShapeDtypeStruct((m, n), F32),
        compiler_params=_params("arbitrary", "arbitrary", "arbitrary"),
        name="mlp_down",
    )(a, w, x2d, gate.reshape(bsz, 1, n))


def _suffix_sum_matrix():
    r = np.arange(2 * LANES)[:, None] % LANES
    c = np.arange(2 * LANES)[None, :]
    return jnp.asarray(np.where(c < LANES, r >= c, True), dtype=BF16)


def _sb_kernel(q_ref, k_ref, v_ref, ut_ref, o_ref, acc_ref, carry_ref, *, tq, tk, scale):
    seq = q_ref.shape[0]
    n_heads = q_ref.shape[1] // HEAD_DIM
    n_diag = tq // tk
    z_scale = scale * LOG2_E
    sign_bit = jnp.uint32(0x80000000)

    def visit(hd, q0, r0, nr, k0, triangle):
        hcols = slice(hd * HEAD_DIM, (hd + 1) * HEAD_DIM)
        rows = slice(r0, r0 + nr)
        q = q_ref[pl.ds(pl.multiple_of(q0 + r0, tk), nr), hcols]
        k = k_ref[pl.ds(k0, tk), hcols]
        v = v_ref[pl.ds(k0, tk), hcols]
        z2 = lax.dot_general(q, k, (((1,), (1,)), ((), ())), preferred_element_type=F32) * z_scale
        neg_abs = lax.bitcast_convert_type(lax.bitcast_convert_type(z2, jnp.uint32) | sign_bit, F32)
        nlr = jnp.maximum(z2, 0.0) + jnp.log2(1.0 + jnp.exp2(neg_abs))
        if triangle:
            row = lax.broadcasted_iota(jnp.int32, (nr, tk), 0)
            col = lax.broadcasted_iota(jnp.int32, (nr, tk), 1)
            causal = col < row
            nlr = jnp.where(causal, nlr, 0.0)
        carry = carry_ref[hd, rows, :]
        parts = [None] * (tk // LANES)
        for sb in reversed(range(tk // LANES)):
            cols = slice(sb * LANES, (sb + 1) * LANES)
            x = nlr[:, cols]
            hi = x.astype(BF16)
            lo = (x - hi.astype(F32)).astype(BF16)
            cs = jnp.dot(jnp.concatenate([hi, lo], axis=1), ut_ref[...], preferred_element_type=F32)
            a = jnp.exp2(z2[:, cols] - cs[:, :LANES] - carry)
            if triangle:
                a = jnp.where(causal[:, cols], a, 0.0)
            parts[sb] = a.astype(BF16)
            carry = carry + cs[:, LANES:]
        carry_ref[hd, rows, :] = carry
        acc_ref[hd, rows, :] += jnp.dot(jnp.concatenate(parts, axis=1), v, preferred_element_type=F32)

    def q_block(iq, _):
        q0 = pl.multiple_of(iq * tq, tq)
        acc_ref[...] = jnp.zeros_like(acc_ref)
        carry_ref[...] = jnp.zeros_like(carry_ref)
        for d in reversed(range(n_diag)):
            for hd in range(n_heads):
                visit(hd, q0, d * tk, tk, q0 + d * tk, True)
                if d < n_diag - 1:
                    visit(hd, q0, (d + 1) * tk, tq - (d + 1) * tk, q0 + d * tk, False)

        def left(t, _):
            k0 = pl.multiple_of(q0 - (t + 1) * tk, tk)
            for hd in range(n_heads):
                visit(hd, q0, 0, tq, k0, False)
            return 0

        lax.fori_loop(0, iq * n_diag, left, 0)
        for hd in range(n_heads):
            o_ref[pl.ds(q0, tq), hd * HEAD_DIM:(hd + 1) * HEAD_DIM] = acc_ref[hd].astype(o_ref.dtype)
        return 0

    lax.fori_loop(0, seq // tq, q_block, 0)


def _stick_breaking(proj, bsz, seq, n_heads, q_col, k_col, v_col, *, tq=512, tk=256, heads_per_step=4):
    hps = min(heads_per_step, n_heads)
    wb = hps * HEAD_DIM
    qb, kb, vb = q_col // wb, k_col // wb, v_col // wb
    kern = functools.partial(_sb_kernel, tq=tq, tk=tk, scale=HEAD_DIM ** -0.5)
    return pl.pallas_call(
        kern,
        grid=(bsz, n_heads // hps),
        in_specs=[pl.BlockSpec((seq, wb), lambda b, h: (b, qb + h)),
                  pl.BlockSpec((seq, wb), lambda b, h: (b, kb + h)),
                  pl.BlockSpec((seq, wb), lambda b, h: (b, vb + h)),
                  pl.BlockSpec((2 * LANES, 2 * LANES), lambda b, h: (0, 0))],
        out_specs=pl.BlockSpec((seq, wb), lambda b, h: (b, h)),
        out_shape=jax.ShapeDtypeStruct((bsz * seq, n_heads * HEAD_DIM), BF16),
        scratch_shapes=[pltpu.VMEM((hps, tq, HEAD_DIM), F32), pltpu.VMEM((hps, tq, LANES), F32)],
        compiler_params=_params("arbitrary", "arbitrary"),
        name="stick_breaking",
    )(proj, proj, proj, _suffix_sum_matrix())


WINDOW = 128
DIL_TILE = 1024
CLASSES_PER_ITER = 4


def _window_softmax(q, k, v, valid, scale):
    s = lax.dot_general(q, k, (((1,), (1,)), ((), ())), preferred_element_type=F32) * scale
    s = jnp.where(valid, s, -jnp.inf)
    m = jnp.max(s, axis=1, keepdims=True)
    p = jnp.exp(s - m)
    l = jnp.sum(p, axis=1, keepdims=True)
    o = jnp.dot(p.astype(BF16), v, preferred_element_type=F32) / l
    return o, m + jnp.log(l)


def _dil_strided_kernel(q_ref, k_ref, v_ref, o_ref, lse_ref, kd_ref, vd_ref, *, dil, scale):
    n_heads, tile, _ = q_ref.shape
    chunk = tile // dil
    n_hist = -(-WINDOW // chunk)
    n_slots = n_hist + 1
    takes = [min(chunk, WINDOW - (back - 1) * chunk) for back in range(1, n_hist + 1)]
    n_keys = WINDOW + chunk
    kt = pl.program_id(1)
    slot = lax.rem(kt, n_slots)

    @pl.when(kt == 0)
    def _():
        kd_ref[...] = jnp.zeros_like(kd_ref)
        vd_ref[...] = jnp.zeros_like(vd_ref)

    def deinterleave(c, _):
        rows = pl.ds(pl.multiple_of(c * chunk, chunk), chunk)
        for h in range(n_heads):
            kd_ref[slot, h, rows, :] = k_ref[h, pl.ds(c, chunk, stride=dil), :].astype(BF16)
            vd_ref[slot, h, rows, :] = v_ref[h, pl.ds(c, chunk, stride=dil), :].astype(BF16)
        return 0

    lax.fori_loop(0, dil, deinterleave, 0)

    row = lax.broadcasted_iota(jnp.int32, (chunk, n_keys), 0)
    col = lax.broadcasted_iota(jnp.int32, (chunk, n_keys), 1)
    valid = jnp.logical_and(col >= row, col <= row + WINDOW)
    valid = jnp.logical_and(valid, col >= WINDOW - chunk * kt)
    lane = lax.broadcasted_iota(jnp.int32, (chunk, LANES), 1)

    def attend_class(c):
        base = pl.multiple_of(c * chunk, chunk)
        lse_blk = jnp.zeros((chunk, LANES), F32)
        for h in range(n_heads):
            q = q_ref[h, pl.ds(c, chunk, stride=dil), :].astype(BF16)
            ks, vs = [], []
            for back in range(n_hist, 0, -1):
                take = takes[back - 1]
                old = lax.rem(kt + (n_slots - back), n_slots)
                rows = pl.ds(pl.multiple_of(base + (chunk - take), 16), take)
                ks.append(kd_ref[old, h, rows, :])
                vs.append(vd_ref[old, h, rows, :])
            ks.append(kd_ref[slot, h, pl.ds(base, chunk), :])
            vs.append(vd_ref[slot, h, pl.ds(base, chunk), :])
            o, lse = _window_softmax(q, jnp.concatenate(ks, axis=0), jnp.concatenate(vs, axis=0),
                                     valid, scale)
            o_ref[h, pl.ds(c, chunk, stride=dil), :] = o
            lse_blk = jnp.where(lane == h, lse, lse_blk)
        lse_ref[pl.ds(c, chunk, stride=dil), :] = lse_blk

    def attend(ci, _):
        for cc in range(CLASSES_PER_ITER):
            attend_class(ci * CLASSES_PER_ITER + cc)
        return 0

    lax.fori_loop(0, dil // CLASSES_PER_ITER, attend, 0)


def _dil_strided(heads_qkv, g, bsz, seq, n_heads):
    n = heads_qkv.shape[1]
    dil = DIL_GROUPS[g][1]
    tile = min(DIL_TILE, seq)
    chunk = tile // dil
    n_slots = -(-WINDOW // chunk) + 1
    per_seq = seq // tile

    def operand(which):
        return pl.BlockSpec((n_heads, tile, HEAD_DIM), lambda b, t: (2 * which + g - 1, b * per_seq + t, 0))

    return pl.pallas_call(
        functools.partial(_dil_strided_kernel, dil=dil, scale=HEAD_DIM ** -0.5),
        grid=(bsz, per_seq),
        in_specs=[operand(0), operand(1), operand(2)],
        out_specs=[pl.BlockSpec((n_heads, tile, HEAD_DIM), lambda b, t: (0, b * per_seq + t, 0)),
                   pl.BlockSpec((tile, LANES), lambda b, t: (b * per_seq + t, 0))],
        out_shape=[jax.ShapeDtypeStruct((n_heads, n, HEAD_DIM), F32),
                   jax.ShapeDtypeStruct((n, LANES), F32)],
        scratch_shapes=[pltpu.VMEM((n_slots, n_heads, tile, HEAD_DIM), BF16),
                        pltpu.VMEM((n_slots, n_heads, tile, HEAD_DIM), BF16)],
        compiler_params=_params("arbitrary", "arbitrary"),
        name=f"dilated_stride{dil}",
    )(heads_qkv, heads_qkv, heads_qkv)


def _dil_merge_kernel(q_ref, kp_ref, kc_ref, vp_ref, vc_ref, o2_ref, l2_ref, o3_ref, l3_ref, y_ref,
                      *, scale):
    tile = q_ref.shape[0]
    n_heads = q_ref.shape[1] // HEAD_DIM
    blk = WINDOW
    first = pl.program_id(1) == 0
    row = lax.broadcasted_iota(jnp.int32, (blk, 2 * blk), 0)
    col = lax.broadcasted_iota(jnp.int32, (blk, 2 * blk), 1)
    band = jnp.logical_and(col >= row, col <= row + blk)
    band_first = jnp.logical_and(band, jnp.logical_or(col >= blk, jnp.logical_not(first)))
    for sub in range(tile // blk):
        rows = slice(sub * blk, (sub + 1) * blk)
        for h in range(n_heads):
            cols = slice(h * HEAD_DIM, (h + 1) * HEAD_DIM)
            if sub == 0:
                k = jnp.concatenate([kp_ref[:, cols], kc_ref[rows, cols]], axis=0)
                v = jnp.concatenate([vp_ref[:, cols], vc_ref[rows, cols]], axis=0)
                valid = band_first
            else:
                k = kc_ref[(sub - 1) * blk:(sub + 1) * blk, cols]
                v = vc_ref[(sub - 1) * blk:(sub + 1) * blk, cols]
                valid = band
            o1, lse1 = _window_softmax(q_ref[rows, cols], k, v, valid, scale)
            lse2 = l2_ref[rows, h:h + 1]
            lse3 = l3_ref[rows, h:h + 1]
            mx = jnp.maximum(jnp.maximum(lse1, lse2), lse3)
            w1, w2, w3 = jnp.exp(lse1 - mx), jnp.exp(lse2 - mx), jnp.exp(lse3 - mx)
            y = (w1 * o1 + w2 * o2_ref[h, rows, :] + w3 * o3_ref[h, rows, :]) / (w1 + w2 + w3)
            y_ref[rows, cols] = y.astype(y_ref.dtype)


def _dil_merge(proj, q_col, k_col, v_col, o2, l2, o3, l3, bsz, seq, n_heads, *, tile=512):
    n = proj.shape[0]
    gw = n_heads * HEAD_DIM
    tile = min(tile, seq)
    per_seq = seq // tile
    sub_per_tile = tile // WINDOW
    sub_per_seq = seq // WINDOW
    qo, ko, vo = q_col // gw, k_col // gw, v_col // gw

    def cur(off):
        return pl.BlockSpec((tile, gw), lambda b, t: (b * per_seq + t, off))

    def prev(off):
        return pl.BlockSpec(
            (WINDOW, gw), lambda b, t: (b * sub_per_seq + jnp.maximum(t * sub_per_tile - 1, 0), off))

    heads = pl.BlockSpec((n_heads, tile, HEAD_DIM), lambda b, t: (0, b * per_seq + t, 0))
    lses = pl.BlockSpec((tile, LANES), lambda b, t: (b * per_seq + t, 0))
    return pl.pallas_call(
        functools.partial(_dil_merge_kernel, scale=HEAD_DIM ** -0.5),
        grid=(bsz, per_seq),
        in_specs=[cur(qo), prev(ko), cur(ko), prev(vo), cur(vo), heads, lses, heads, lses],
        out_specs=pl.BlockSpec((tile, gw), lambda b, t: (b * per_seq + t, 0)),
        out_shape=jax.ShapeDtypeStruct((n, gw), BF16),
        compiler_params=_params("arbitrary", "arbitrary"),
        name="dilated_merge",
    )(proj, proj, proj, proj, proj, o2, l2, o3, l3)


def _dilated_attention(proj, q_col, k_col, v_col, heads_qkv, bsz, seq, n_heads):
    o3, l3 = _dil_strided(heads_qkv, 2, bsz, seq, n_heads)
    o2, l2 = _dil_strided(heads_qkv, 1, bsz, seq, n_heads)
    return _dil_merge(proj, q_col, k_col, v_col, o2, l2, o3, l3, bsz, seq, n_heads)


def kernel(x, c, positions, ada_w, ada_b, norm_mix_w, w_in, w_o_sb, w_o_dil, w_out,
           norm_mlp_w, w_ff1, w_ff2, norm_final_w):
    bsz, seq, d = x.shape
    n = bsz * seq
    d_sb = w_o_sb.shape[1]
    d_dil = w_o_dil.shape[1]
    sb_heads = d_sb // HEAD_DIM
    dil_heads = d_dil // HEAD_DIM
    bn = d_dil
    sb_t = 3 * d_sb // bn
    dil_t = lambda which, g: sb_t + which * N_DIL + g
    gate_t = sb_t + 3 * N_DIL
    main_tiles = (list(range(sb_t)) + [dil_t(0, 0), dil_t(1, 0), dil_t(2, 0)]
                  + list(range(gate_t, gate_t + 2 * d // bn)))
    main_rope = [0] * sb_t + [1, 1, 0] + [0] * (2 * d // bn)
    q0_col, k0_col, v0_col = sb_t * bn, (sb_t + 1) * bn, (sb_t + 2) * bn
    g_sb_col = (sb_t + 3) * bn
    g_dil_col = g_sb_col + d
    head_tiles = [dil_t(which, g) for which in range(3) for g in (1, 2)]
    head_rope = [1, 1, 1, 1, 0, 0]

    cos, sin = _rope_tables(positions)
    xf = x.reshape(n, d)
    for l in range(ada_w.shape[0]):
        mod = _ada_modulation(c, ada_w[l], ada_b[l])
        sh1, sc1, g1, sh2, sc2, g2 = [mod[:, i * d:(i + 1) * d] for i in range(6)]

        h = _norm_mod(xf, norm_mix_w[l], sc1, sh1, seq)
        w_in_l = w_in[l].astype(BF16)
        proj = _in_projection(h, w_in_l, cos, sin, main_tiles, main_rope, head_major=False, bn=bn)
        heads_qkv = _in_projection(h, w_in_l, cos, sin, head_tiles, head_rope, head_major=True, bn=bn)
        y_a = _stick_breaking(proj, bsz, seq, sb_heads, 0, d_sb, 2 * d_sb)
        y_b = _dilated_attention(proj, q0_col, k0_col, v0_col, heads_qkv, bsz, seq, dil_heads)
        merged = _gated_merge(y_a, w_o_sb[l].astype(BF16), y_b, w_o_dil[l].astype(BF16),
                              proj, g_sb_col, g_dil_col)
        xf = _residual_matmul(merged, w_out[l].astype(BF16), xf, g1, seq)

        h = _norm_mod(xf, norm_mlp_w[l], sc2, sh2, seq)
        u = _relu2_matmul(h, w_ff1[l].astype(BF16))
        xf = _residual_matmul_ksplit(u, w_ff2[l].astype(BF16), xf, g2, seq)
    return _final_norm(xf, norm_final_w).reshape(bsz, seq, d)
```

```python
import functools

import jax
import jax.numpy as jnp
import numpy as np
from jax import lax
from jax.experimental import pallas as pl
from jax.experimental.pallas import tpu as pltpu

F32 = jnp.float32
BF16 = jnp.bfloat16

HEAD_DIM = 128
LANES = 128
DIL_GROUPS = ((128, 1), (512, 4), (2048, 16))
N_DIL = len(DIL_GROUPS)
ROPE_THETA = 10000.0
EPS = 1e-6
LOG2_E = 1.4426950408889634
VMEM_LIMIT = 56 * 1024 * 1024


def _params(*sem):
    return pltpu.CompilerParams(dimension_semantics=sem, vmem_limit_bytes=VMEM_LIMIT)


def _ada_kernel(c_ref, w_ref, b_ref, o_ref):
    c = c_ref[...]
    s = c * (1.0 / (1.0 + jnp.exp(-c)))
    o_ref[...] = jnp.dot(s.astype(BF16), w_ref[...].astype(BF16),
                         preferred_element_type=F32) + b_ref[...]


def _ada_modulation(c, w, b, *, bn=512):
    bsz, d = c.shape
    n = w.shape[1]
    rows = 8
    c_pad = jnp.zeros((rows, d), F32).at[:bsz].set(c)
    out = pl.pallas_call(
        _ada_kernel,
        grid=(n // bn,),
        in_specs=[pl.BlockSpec((rows, d), lambda j: (0, 0)),
                  pl.BlockSpec((d, bn), lambda j: (0, j)),
                  pl.BlockSpec((1, bn), lambda j: (0, j))],
        out_specs=pl.BlockSpec((rows, bn), lambda j: (0, j)),
        out_shape=jax.ShapeDtypeStruct((rows, n), F32),
        compiler_params=_params("arbitrary"),
        name="ada_modulation",
    )(c_pad, w, b.reshape(1, n))
    return out[:bsz]


def _norm_mod_kernel(x_ref, w_ref, sc_ref, sh_ref, o_ref):
    x = x_ref[...]
    inv = lax.rsqrt(jnp.mean(x * x, axis=-1, keepdims=True) + EPS)
    mul = w_ref[...] * (1.0 + sc_ref[...])
    o_ref[...] = (x * inv * mul + sh_ref[...]).astype(o_ref.dtype)


def _norm_mod(x2d, w, sc, sh, seq, *, tm=256):
    n, d = x2d.shape
    bsz = sc.shape[0]
    per_seq = seq // tm
    return pl.pallas_call(
        _norm_mod_kernel,
        grid=(n // tm,),
        in_specs=[pl.BlockSpec((tm, d), lambda i: (i, 0)),
                  pl.BlockSpec((1, d), lambda i: (0, 0)),
                  pl.BlockSpec((None, 1, d), lambda i: (i // per_seq, 0, 0)),
                  pl.BlockSpec((None, 1, d), lambda i: (i // per_seq, 0, 0))],
        out_specs=pl.BlockSpec((tm, d), lambda i: (i, 0)),
        out_shape=jax.ShapeDtypeStruct((n, d), BF16),
        compiler_params=_params("arbitrary"),
        name="norm_modulate",
    )(x2d, w.reshape(1, d), sc.reshape(bsz, 1, d), sh.reshape(bsz, 1, d))


def _final_norm_kernel(x_ref, w_ref, o_ref):
    x = x_ref[...]
    inv = lax.rsqrt(jnp.mean(x * x, axis=-1, keepdims=True) + EPS)
    o_ref[...] = x * inv * w_ref[...]


def _final_norm(x2d, w, *, tm=256):
    n, d = x2d.shape
    return pl.pallas_call(
        _final_norm_kernel,
        grid=(n // tm,),
        in_specs=[pl.BlockSpec((tm, d), lambda i: (i, 0)),
                  pl.BlockSpec((1, d), lambda i: (0, 0))],
        out_specs=pl.BlockSpec((tm, d), lambda i: (i, 0)),
        out_shape=jax.ShapeDtypeStruct((n, d), F32),
        compiler_params=_params("arbitrary"),
        name="final_norm",
    )(x2d, w.reshape(1, d))


def _rope_table_kernel(pos_ref, invf_ref, sign_ref, cos_ref, sin_ref):
    ang = pos_ref[...].astype(F32) * invf_ref[...]
    cos_ref[...] = jnp.cos(ang)
    sin_ref[...] = jnp.sin(ang) * sign_ref[...]


def _rope_tables(positions, *, tm=2048):
    n = positions.size
    half = HEAD_DIM // 2
    inv_freq = ROPE_THETA ** (-jnp.arange(half, dtype=F32) / half)
    invf = jnp.concatenate([inv_freq, inv_freq]).reshape(1, HEAD_DIM)
    sign = jnp.concatenate([-jnp.ones((half,), F32), jnp.ones((half,), F32)]).reshape(1, HEAD_DIM)
    tm = min(tm, n)
    return pl.pallas_call(
        _rope_table_kernel,
        grid=(n // tm,),
        in_specs=[pl.BlockSpec((tm, 1), lambda i: (i, 0)),
                  pl.BlockSpec((1, HEAD_DIM), lambda i: (0, 0)),
                  pl.BlockSpec((1, HEAD_DIM), lambda i: (0, 0))],
        out_specs=[pl.BlockSpec((tm, HEAD_DIM), lambda i: (i, 0)),
                   pl.BlockSpec((tm, HEAD_DIM), lambda i: (i, 0))],
        out_shape=[jax.ShapeDtypeStruct((n, HEAD_DIM), F32),
                   jax.ShapeDtypeStruct((n, HEAD_DIM), F32)],
        compiler_params=_params("arbitrary"),
        name="rope_tables",
    )(positions.reshape(n, 1), invf, sign)


def _inproj_kernel(tile_ref, rope_ref, a_ref, w_ref, cos_ref, sin_ref, o_ref, *, head_major):
    del tile_ref
    rope = rope_ref[pl.program_id(1)] == 1
    acc = jnp.dot(a_ref[...], w_ref[...], preferred_element_type=F32)
    cos = cos_ref[...]
    sin = sin_ref[...]
    for h in range(acc.shape[1] // HEAD_DIM):
        xh = acc[:, h * HEAD_DIM:(h + 1) * HEAD_DIM]
        val = jnp.where(rope, xh * cos + pltpu.roll(xh, HEAD_DIM // 2, 1) * sin, xh)
        if head_major:
            o_ref[h] = val.astype(o_ref.dtype)
        else:
            o_ref[:, h * HEAD_DIM:(h + 1) * HEAD_DIM] = val.astype(o_ref.dtype)


def _in_projection(h, w, cos, sin, col_tiles, rope_flags, *, head_major, bm=1024, bn=1024):
    m, k = h.shape
    nt = len(col_tiles)
    if head_major:
        out_spec = pl.BlockSpec((bn // HEAD_DIM, bm, HEAD_DIM), lambda i, j, t, r: (j, i, 0))
        out_shape = jax.ShapeDtypeStruct((nt * (bn // HEAD_DIM), m, HEAD_DIM), F32)
    else:
        out_spec = pl.BlockSpec((bm, bn), lambda i, j, t, r: (i, j))
        out_shape = jax.ShapeDtypeStruct((m, nt * bn), BF16)
    return pl.pallas_call(
        functools.partial(_inproj_kernel, head_major=head_major),
        grid_spec=pltpu.PrefetchScalarGridSpec(
            num_scalar_prefetch=2,
            grid=(m // bm, nt),
            in_specs=[pl.BlockSpec((bm, k), lambda i, j, t, r: (i, 0)),
                      pl.BlockSpec((k, bn), lambda i, j, t, r: (0, t[j])),
                      pl.BlockSpec((bm, HEAD_DIM), lambda i, j, t, r: (i, 0)),
                      pl.BlockSpec((bm, HEAD_DIM), lambda i, j, t, r: (i, 0))],
            out_specs=out_spec),
        out_shape=out_shape,
        compiler_params=_params("arbitrary", "arbitrary"),
        name="in_projection_heads" if head_major else "in_projection",
    )(jnp.asarray(col_tiles, jnp.int32), jnp.asarray(rope_flags, jnp.int32), h, w, cos, sin)


def _merge_kernel(ya_ref, wa_ref, yb_ref, wb_ref, ga_ref, gb_ref, o_ref):
    a = jnp.dot(ya_ref[...], wa_ref[...], preferred_element_type=F32)
    b = jnp.dot(yb_ref[...], wb_ref[...], preferred_element_type=F32)
    ga = ga_ref[...].astype(F32)
    gb = gb_ref[...].astype(F32)
    o_ref[...] = (a / (1.0 + jnp.exp(-ga)) + b / (1.0 + jnp.exp(-gb))).astype(o_ref.dtype)


def _gated_merge(ya, wa, yb, wb, proj, ga_col, gb_col, *, bm=1024, bn=1024):
    m = ya.shape[0]
    n = wa.shape[1]
    ga_blk, gb_blk = ga_col // bn, gb_col // bn
    return pl.pallas_call(
        _merge_kernel,
        grid=(m // bm, n // bn),
        in_specs=[pl.BlockSpec((bm, ya.shape[1]), lambda i, j: (i, 0)),
                  pl.BlockSpec((wa.shape[0], bn), lambda i, j: (0, j)),
                  pl.BlockSpec((bm, yb.shape[1]), lambda i, j: (i, 0)),
                  pl.BlockSpec((wb.shape[0], bn), lambda i, j: (0, j)),
                  pl.BlockSpec((bm, bn), lambda i, j: (i, ga_blk + j)),
                  pl.BlockSpec((bm, bn), lambda i, j: (i, gb_blk + j))],
        out_specs=pl.BlockSpec((bm, bn), lambda i, j: (i, j)),
        out_shape=jax.ShapeDtypeStruct((m, n), BF16),
        compiler_params=_params("arbitrary", "arbitrary"),
        name="gated_merge",
    )(ya, wa, yb, wb, proj, proj)


def _residual_mm_kernel(a_ref, w_ref, x_ref, g_ref, o_ref):
    acc = jnp.dot(a_ref[...], w_ref[...], preferred_element_type=F32)
    o_ref[...] = x_ref[...] + g_ref[...] * acc


def _residual_matmul(a, w, x2d, gate, seq, *, bm=1024, bn=1024):
    m, k = a.shape
    n = w.shape[1]
    bsz = gate.shape[0]
    per_seq = seq // bm
    return pl.pallas_call(
        _residual_mm_kernel,
        grid=(m // bm, n // bn),
        in_specs=[pl.BlockSpec((bm, k), lambda i, j: (i, 0)),
                  pl.BlockSpec((k, bn), lambda i, j: (0, j)),
                  pl.BlockSpec((bm, bn), lambda i, j: (i, j)),
                  pl.BlockSpec((None, 1, bn), lambda i, j: (i // per_seq, 0, j))],
        out_specs=pl.BlockSpec((bm, bn), lambda i, j: (i, j)),
        out_shape=jax.ShapeDtypeStruct((m, n), F32),
        compiler_params=_params("arbitrary", "arbitrary"),
        name="out_projection",
    )(a, w, x2d, gate.reshape(bsz, 1, n))


def _relu2_mm_kernel(a_ref, w_ref, o_ref):
    acc = jnp.dot(a_ref[...], w_ref[...], preferred_element_type=F32)
    r = jnp.maximum(acc, 0.0)
    o_ref[...] = (r * r).astype(o_ref.dtype)


def _relu2_matmul(a, w, *, bm=1024, bn=1024):
    m, k = a.shape
    n = w.shape[1]
    return pl.pallas_call(
        _relu2_mm_kernel,
        grid=(m // bm, n // bn),
        in_specs=[pl.BlockSpec((bm, k), lambda i, j: (i, 0)),
                  pl.BlockSpec((k, bn), lambda i, j: (0, j))],
        out_specs=pl.BlockSpec((bm, bn), lambda i, j: (i, j)),
        out_shape=jax.ShapeDtypeStruct((m, n), BF16),
        compiler_params=_params("arbitrary", "arbitrary"),
        name="mlp_up",
    )(a, w)


def _residual_mm_ksplit_kernel(a_ref, w_ref, x_ref, g_ref, o_ref):
    kk = pl.program_id(2)

    @pl.when(kk == 0)
    def _():
        o_ref[...] = jnp.zeros_like(o_ref)

    o_ref[...] += jnp.dot(a_ref[...], w_ref[...], preferred_element_type=F32)

    @pl.when(kk == pl.num_programs(2) - 1)
    def _():
        o_ref[...] = x_ref[...] + g_ref[...] * o_ref[...]


def _residual_matmul_ksplit(a, w, x2d, gate, seq, *, bm=1024, bn=1024, bk=4096):
    m, k = a.shape
    n = w.shape[1]
    bsz = gate.shape[0]
    per_seq = seq // bm
    return pl.pallas_call(
        _residual_mm_ksplit_kernel,
        grid=(m // bm, n // bn, k // bk),
        in_specs=[pl.BlockSpec((bm, bk), lambda i, j, kk: (i, kk)),
                  pl.BlockSpec((bk, bn), lambda i, j, kk: (kk, j)),
                  pl.BlockSpec((bm, bn), lambda i, j, kk: (i, j)),
                  pl.BlockSpec((None, 1, bn), lambda i, j, kk: (i // per_seq, 0, j))],
        out_specs=pl.BlockSpec((bm, bn), lambda i, j, kk: (i, j)),
        out_shape=jax.ShapeDtypeStruct((m, n), F32),
        compiler_params=_params("arbitrary", "arbitrary", "arbitrary"),
        name="mlp_down",
    )(a, w, x2d, gate.reshape(bsz, 1, n))


def _suffix_sum_matrix():
    r = np.arange(2 * LANES)[:, None] % LANES
    c = np.arange(2 * LANES)[None, :]
    return jnp.asarray(np.where(c < LANES, r >= c, True), dtype=BF16)


def _sb_kernel(q_ref, k_ref, v_ref, ut_ref, o_ref, acc_ref, carry_ref, *, tq, tk, scale):
    seq = q_ref.shape[0]
    n_heads = q_ref.shape[1] // HEAD_DIM
    n_diag = tq // tk
    z_scale = scale * LOG2_E
    sign_bit = jnp.uint32(0x80000000)

    def visit(hd, q0, r0, nr, k0, triangle):
        hcols = slice(hd * HEAD_DIM, (hd + 1) * HEAD_DIM)
        rows = slice(r0, r0 + nr)
        q = q_ref[pl.ds(pl.multiple_of(q0 + r0, tk), nr), hcols]
        k = k_ref[pl.ds(k0, tk), hcols]
        v = v_ref[pl.ds(k0, tk), hcols]
        z2 = lax.dot_general(q, k, (((1,), (1,)), ((), ())), preferred_element_type=F32) * z_scale
        neg_abs = lax.bitcast_convert_type(lax.bitcast_convert_type(z2, jnp.uint32) | sign_bit, F32)
        nlr = jnp.maximum(z2, 0.0) + jnp.log2(1.0 + jnp.exp2(neg_abs))
        if triangle:
            row = lax.broadcasted_iota(jnp.int32, (nr, tk), 0)
            col = lax.broadcasted_iota(jnp.int32, (nr, tk), 1)
            causal = col < row
            nlr = jnp.where(causal, nlr, 0.0)
        carry = carry_ref[hd, rows, :]
        parts = [None] * (tk // LANES)
        for sb in reversed(range(tk // LANES)):
            cols = slice(sb * LANES, (sb + 1) * LANES)
            x = nlr[:, cols]
            hi = x.astype(BF16)
            lo = (x - hi.astype(F32)).astype(BF16)
            cs = jnp.dot(jnp.concatenate([hi, lo], axis=1), ut_ref[...], preferred_element_type=F32)
            a = jnp.exp2(z2[:, cols] - cs[:, :LANES] - carry)
            if triangle:
                a = jnp.where(causal[:, cols], a, 0.0)
            parts[sb] = a.astype(BF16)
            carry = carry + cs[:, LANES:]
        carry_ref[hd, rows, :] = carry
        acc_ref[hd, rows, :] += jnp.dot(jnp.concatenate(parts, axis=1), v, preferred_element_type=F32)

    def q_block(iq, _):
        q0 = pl.multiple_of(iq * tq, tq)
        acc_ref[...] = jnp.zeros_like(acc_ref)
        carry_ref[...] = jnp.zeros_like(carry_ref)
        for d in reversed(range(n_diag)):
            for hd in range(n_heads):
                visit(hd, q0, d * tk, tk, q0 + d * tk, True)
                if d < n_diag - 1:
                    visit(hd, q0, (d + 1) * tk, tq - (d + 1) * tk, q0 + d * tk, False)

        def left(t, _):
            k0 = pl.multiple_of(q0 - (t + 1) * tk, tk)
            for hd in range(n_heads):
                visit(hd, q0, 0, tq, k0, False)
            return 0

        lax.fori_loop(0, iq * n_diag, left, 0)
        for hd in range(n_heads):
            o_ref[pl.ds(q0, tq), hd * HEAD_DIM:(hd + 1) * HEAD_DIM] = acc_ref[hd].astype(o_ref.dtype)
        return 0

    lax.fori_loop(0, seq // tq, q_block, 0)


def _stick_breaking(proj, bsz, seq, n_heads, q_col, k_col, v_col, *, tq=512, tk=256, heads_per_step=4):
    hps = min(heads_per_step, n_heads)
    wb = hps * HEAD_DIM
    qb, kb, vb = q_col // wb, k_col // wb, v_col // wb
    kern = functools.partial(_sb_kernel, tq=tq, tk=tk, scale=HEAD_DIM ** -0.5)
    return pl.pallas_call(
        kern,
        grid=(bsz, n_heads // hps),
        in_specs=[pl.BlockSpec((seq, wb), lambda b, h: (b, qb + h)),
                  pl.BlockSpec((seq, wb), lambda b, h: (b, kb + h)),
                  pl.BlockSpec((seq, wb), lambda b, h: (b, vb + h)),
                  pl.BlockSpec((2 * LANES, 2 * LANES), lambda b, h: (0, 0))],
        out_specs=pl.BlockSpec((seq, wb), lambda b, h: (b, h)),
        out_shape=jax.ShapeDtypeStruct((bsz * seq, n_heads * HEAD_DIM), BF16),
        scratch_shapes=[pltpu.VMEM((hps, tq, HEAD_DIM), F32), pltpu.VMEM((hps, tq, LANES), F32)],
        compiler_params=_params("arbitrary", "arbitrary"),
        name="stick_breaking",
    )(proj, proj, proj, _suffix_sum_matrix())


WINDOW = 128
DIL_TILE = 1024
MIN_PROBLEM_ROWS = 256


def _window_softmax(q, k, v, valid, scale):
    s = lax.dot_general(q, k, (((1,), (1,)), ((), ())), preferred_element_type=F32) * scale
    s = jnp.where(valid, s, -jnp.inf)
    m = jnp.max(s, axis=1, keepdims=True)
    p = jnp.exp(s - m)
    l = jnp.sum(p, axis=1, keepdims=True)
    o = jnp.dot(p.astype(BF16), v, preferred_element_type=F32) / l
    return o, m + jnp.log(l)


def _dil_strided_kernel(q_ref, k_ref, v_ref, o_ref, lse_ref, qd_ref, kd_ref, vd_ref, *, dil, group, scale):
    n_heads, tile, _ = q_ref.shape
    chunk = tile // dil
    n_hist = -(-WINDOW // chunk)
    n_slots = n_hist + 1
    takes = [min(chunk, WINDOW - (back - 1) * chunk) for back in range(1, n_hist + 1)]
    assert group == 1 or all(t == chunk for t in takes)
    seg = group * chunk
    n_keys = group * (WINDOW + chunk)
    kt = pl.program_id(1)
    slot = lax.rem(kt, n_slots)

    @pl.when(kt == 0)
    def _():
        kd_ref[...] = jnp.zeros_like(kd_ref)
        vd_ref[...] = jnp.zeros_like(vd_ref)

    def deinterleave(c, _):
        rows = pl.ds(pl.multiple_of(c * chunk, chunk), chunk)
        for h in range(n_heads):
            qd_ref[h, rows, :] = q_ref[h, pl.ds(c, chunk, stride=dil), :].astype(BF16)
            kd_ref[slot, h, rows, :] = k_ref[h, pl.ds(c, chunk, stride=dil), :].astype(BF16)
            vd_ref[slot, h, rows, :] = v_ref[h, pl.ds(c, chunk, stride=dil), :].astype(BF16)
        return 0

    lax.fori_loop(0, dil, deinterleave, 0)

    row = lax.broadcasted_iota(jnp.int32, (seg, n_keys), 0)
    col = lax.broadcasted_iota(jnp.int32, (seg, n_keys), 1)
    if group == 1:
        i, u, same_class = row, col, None
    else:
        lc, ls = chunk.bit_length() - 1, seg.bit_length() - 1
        i = row & (chunk - 1)
        within = col & (seg - 1)
        u = ((col >> ls) << lc) + (within & (chunk - 1))
        same_class = (within >> lc) == (row >> lc)
    valid = jnp.logical_and(u >= i, u <= i + WINDOW)
    valid = jnp.logical_and(valid, u >= WINDOW - chunk * kt)
    if same_class is not None:
        valid = jnp.logical_and(valid, same_class)
    lane = lax.broadcasted_iota(jnp.int32, (seg, LANES), 1)

    def attend(gi, _):
        base = pl.multiple_of(gi * seg, seg)
        lse_blk = jnp.zeros((seg, LANES), F32)
        for h in range(n_heads):
            ks, vs = [], []
            for back in range(n_hist, 0, -1):
                take = takes[back - 1]
                old = lax.rem(kt + (n_slots - back), n_slots)
                rows = pl.ds(pl.multiple_of(base + (chunk - take), 16), take if group == 1 else seg)
                ks.append(kd_ref[old, h, rows, :])
                vs.append(vd_ref[old, h, rows, :])
            ks.append(kd_ref[slot, h, pl.ds(base, seg), :])
            vs.append(vd_ref[slot, h, pl.ds(base, seg), :])
            o, lse = _window_softmax(qd_ref[h, pl.ds(base, seg), :], jnp.concatenate(ks, axis=0),
                                     jnp.concatenate(vs, axis=0), valid, scale)
            for cc in range(group):
                o_ref[h, pl.ds(gi * group + cc, chunk, stride=dil), :] = o[cc * chunk:(cc + 1) * chunk]
            lse_blk = jnp.where(lane == h, lse, lse_blk)
        for cc in range(group):
            lse_ref[pl.ds(gi * group + cc, chunk, stride=dil), :] = lse_blk[cc * chunk:(cc + 1) * chunk]
        return 0

    lax.fori_loop(0, dil // group, attend, 0)


def _dil_strided(heads_qkv, g, bsz, seq, n_heads):
    n = heads_qkv.shape[1]
    dil = DIL_GROUPS[g][1]
    tile = min(DIL_TILE, seq)
    chunk = tile // dil
    n_slots = -(-WINDOW // chunk) + 1
    group = max(1, min(dil, MIN_PROBLEM_ROWS // chunk))
    per_seq = seq // tile

    def operand(which):
        return pl.BlockSpec((n_heads, tile, HEAD_DIM), lambda b, t: (2 * which + g - 1, b * per_seq + t, 0))

    return pl.pallas_call(
        functools.partial(_dil_strided_kernel, dil=dil, group=group, scale=HEAD_DIM ** -0.5),
        grid=(bsz, per_seq),
        in_specs=[operand(0), operand(1), operand(2)],
        out_specs=[pl.BlockSpec((n_heads, tile, HEAD_DIM), lambda b, t: (0, b * per_seq + t, 0)),
                   pl.BlockSpec((tile, LANES), lambda b, t: (b * per_seq + t, 0))],
        out_shape=[jax.ShapeDtypeStruct((n_heads, n, HEAD_DIM), F32),
                   jax.ShapeDtypeStruct((n, LANES), F32)],
        scratch_shapes=[pltpu.VMEM((n_heads, tile, HEAD_DIM), BF16),
                        pltpu.VMEM((n_slots, n_heads, tile, HEAD_DIM), BF16),
                        pltpu.VMEM((n_slots, n_heads, tile, HEAD_DIM), BF16)],
        compiler_params=_params("arbitrary", "arbitrary"),
        name=f"dilated_stride{dil}",
    )(heads_qkv, heads_qkv, heads_qkv)


def _dil_merge_kernel(q_ref, kp_ref, kc_ref, vp_ref, vc_ref, o2_ref, l2_ref, o3_ref, l3_ref, y_ref,
                      *, scale):
    tile = q_ref.shape[0]
    n_heads = q_ref.shape[1] // HEAD_DIM
    blk = WINDOW
    first = pl.program_id(1) == 0
    row = lax.broadcasted_iota(jnp.int32, (blk, 2 * blk), 0)
    col = lax.broadcasted_iota(jnp.int32, (blk, 2 * blk), 1)
    band = jnp.logical_and(col >= row, col <= row + blk)
    band_first = jnp.logical_and(band, jnp.logical_or(col >= blk, jnp.logical_not(first)))
    for sub in range(tile // blk):
        rows = slice(sub * blk, (sub + 1) * blk)
        for h in range(n_heads):
            cols = slice(h * HEAD_DIM, (h + 1) * HEAD_DIM)
            if sub == 0:
                k = jnp.concatenate([kp_ref[:, cols], kc_ref[rows, cols]], axis=0)
                v = jnp.concatenate([vp_ref[:, cols], vc_ref[rows, cols]], axis=0)
                valid = band_first
            else:
                k = kc_ref[(sub - 1) * blk:(sub + 1) * blk, cols]
                v = vc_ref[(sub - 1) * blk:(sub + 1) * blk, cols]
                valid = band
            o1, lse1 = _window_softmax(q_ref[rows, cols], k, v, valid, scale)
            lse2 = l2_ref[rows, h:h + 1]
            lse3 = l3_ref[rows, h:h + 1]
            mx = jnp.maximum(jnp.maximum(lse1, lse2), lse3)
            w1, w2, w3 = jnp.exp(lse1 - mx), jnp.exp(lse2 - mx), jnp.exp(lse3 - mx)
            y = (w1 * o1 + w2 * o2_ref[h, rows, :] + w3 * o3_ref[h, rows, :]) / (w1 + w2 + w3)
            y_ref[rows, cols] = y.astype(y_ref.dtype)


def _dil_merge(proj, q_col, k_col, v_col, o2, l2, o3, l3, bsz, seq, n_heads, *, tile=512):
    n = proj.shape[0]
    gw = n_heads * HEAD_DIM
    tile = min(tile, seq)
    per_seq = seq // tile
    sub_per_tile = tile // WINDOW
    sub_per_seq = seq // WINDOW
    qo, ko, vo = q_col // gw, k_col // gw, v_col // gw

    def cur(off):
        return pl.BlockSpec((tile, gw), lambda b, t: (b * per_seq + t, off))

    def prev(off):
        return pl.BlockSpec(
            (WINDOW, gw), lambda b, t: (b * sub_per_seq + jnp.maximum(t * sub_per_tile - 1, 0), off))

    heads = pl.BlockSpec((n_heads, tile, HEAD_DIM), lambda b, t: (0, b * per_seq + t, 0))
    lses = pl.BlockSpec((tile, LANES), lambda b, t: (b * per_seq + t, 0))
    return pl.pallas_call(
        functools.partial(_dil_merge_kernel, scale=HEAD_DIM ** -0.5),
        grid=(bsz, per_seq),
        in_specs=[cur(qo), prev(ko), cur(ko), prev(vo), cur(vo), heads, lses, heads, lses],
        out_specs=pl.BlockSpec((tile, gw), lambda b, t: (b * per_seq + t, 0)),
        out_shape=jax.ShapeDtypeStruct((n, gw), BF16),
        compiler_params=_params("arbitrary", "arbitrary"),
        name="dilated_merge",
    )(proj, proj, proj, proj, proj, o2, l2, o3, l3)


def _dilated_attention(proj, q_col, k_col, v_col, heads_qkv, bsz, seq, n_heads):
    o3, l3 = _dil_strided(heads_qkv, 2, bsz, seq, n_heads)
    o2, l2 = _dil_strided(heads_qkv, 1, bsz, seq, n_heads)
    return _dil_merge(proj, q_col, k_col, v_col, o2, l2, o3, l3, bsz, seq, n_heads)


def kernel(x, c, positions, ada_w, ada_b, norm_mix_w, w_in, w_o_sb, w_o_dil, w_out,
           norm_mlp_w, w_ff1, w_ff2, norm_final_w):
    bsz, seq, d = x.shape
    n = bsz * seq
    d_sb = w_o_sb.shape[1]
    d_dil = w_o_dil.shape[1]
    sb_heads = d_sb // HEAD_DIM
    dil_heads = d_dil // HEAD_DIM
    bn = d_dil
    sb_t = 3 * d_sb // bn
    dil_t = lambda which, g: sb_t + which * N_DIL + g
    gate_t = sb_t + 3 * N_DIL
    main_tiles = (list(range(sb_t)) + [dil_t(0, 0), dil_t(1, 0), dil_t(2, 0)]
                  + list(range(gate_t, gate_t + 2 * d // bn)))
    main_rope = [0] * sb_t + [1, 1, 0] + [0] * (2 * d // bn)
    q0_col, k0_col, v0_col = sb_t * bn, (sb_t + 1) * bn, (sb_t + 2) * bn
    g_sb_col = (sb_t + 3) * bn
    g_dil_col = g_sb_col + d
    head_tiles = [dil_t(which, g) for which in range(3) for g in (1, 2)]
    head_rope = [1, 1, 1, 1, 0, 0]

    cos, sin = _rope_tables(positions)
    xf = x.reshape(n, d)
    for l in range(ada_w.shape[0]):
        mod = _ada_modulation(c, ada_w[l], ada_b[l])
        sh1, sc1, g1, sh2, sc2, g2 = [mod[:, i * d:(i + 1) * d] for i in range(6)]

        h = _norm_mod(xf, norm_mix_w[l], sc1, sh1, seq)
        w_in_l = w_in[l].astype(BF16)
        proj = _in_projection(h, w_in_l, cos, sin, main_tiles, main_rope, head_major=False, bn=bn)
        heads_qkv = _in_projection(h, w_in_l, cos, sin, head_tiles, head_rope, head_major=True, bn=bn)
        y_a = _stick_breaking(proj, bsz, seq, sb_heads, 0, d_sb, 2 * d_sb)
        y_b = _dilated_attention(proj, q0_col, k0_col, v0_col, heads_qkv, bsz, seq, dil_heads)
        merged = _gated_merge(y_a, w_o_sb[l].astype(BF16), y_b, w_o_dil[l].astype(BF16),
                              proj, g_sb_col, g_dil_col)
        xf = _residual_matmul(merged, w_out[l].astype(BF16), xf, g1, seq)

        h = _norm_mod(xf, norm_mlp_w[l], sc2, sh2, seq)
        u = _relu2_matmul(h, w_ff1[l].astype(BF16))
        xf = _residual_matmul_ksplit(u, w_ff2[l].astype(BF16), xf, g2, seq)
    return _final_norm(xf, norm_final_w).reshape(bsz, seq, d)
```

```python
import functools

import jax
import jax.numpy as jnp
import numpy as np
from jax import lax
from jax.experimental import pallas as pl
from jax.experimental.pallas import tpu as pltpu

F32 = jnp.float32
BF16 = jnp.bfloat16

HEAD_DIM = 128
LANES = 128
DIL_GROUPS = ((128, 1), (512, 4), (2048, 16))
N_DIL = len(DIL_GROUPS)
ROPE_THETA = 10000.0
EPS = 1e-6
LOG2_E = 1.4426950408889634
VMEM_LIMIT = 56 * 1024 * 1024


def _params(*sem):
    return pltpu.CompilerParams(dimension_semantics=sem, vmem_limit_bytes=VMEM_LIMIT)


def _ada_kernel(c_ref, w_ref, b_ref, o_ref):
    c = c_ref[...]
    s = c * (1.0 / (1.0 + jnp.exp(-c)))
    o_ref[...] = jnp.dot(s.astype(BF16), w_ref[...].astype(BF16),
                         preferred_element_type=F32) + b_ref[...]


def _ada_modulation(c, w, b, *, bn=512):
    bsz, d = c.shape
    n = w.shape[1]
    rows = 8
    c_pad = jnp.zeros((rows, d), F32).at[:bsz].set(c)
    out = pl.pallas_call(
        _ada_kernel,
        grid=(n // bn,),
        in_specs=[pl.BlockSpec((rows, d), lambda j: (0, 0)),
                  pl.BlockSpec((d, bn), lambda j: (0, j)),
                  pl.BlockSpec((1, bn), lambda j: (0, j))],
        out_specs=pl.BlockSpec((rows, bn), lambda j: (0, j)),
        out_shape=jax.ShapeDtypeStruct((rows, n), F32),
        compiler_params=_params("arbitrary"),
        name="ada_modulation",
    )(c_pad, w, b.reshape(1, n))
    return out[:bsz]


def _norm_mod_kernel(x_ref, w_ref, sc_ref, sh_ref, o_ref):
    x = x_ref[...]
    inv = lax.rsqrt(jnp.mean(x * x, axis=-1, keepdims=True) + EPS)
    mul = w_ref[...] * (1.0 + sc_ref[...])
    o_ref[...] = (x * inv * mul + sh_ref[...]).astype(o_ref.dtype)


def _norm_mod(x2d, w, sc, sh, seq, *, tm=256):
    n, d = x2d.shape
    bsz = sc.shape[0]
    per_seq = seq // tm
    return pl.pallas_call(
        _norm_mod_kernel,
        grid=(n // tm,),
        in_specs=[pl.BlockSpec((tm, d), lambda i: (i, 0)),
                  pl.BlockSpec((1, d), lambda i: (0, 0)),
                  pl.BlockSpec((None, 1, d), lambda i: (i // per_seq, 0, 0)),
                  pl.BlockSpec((None, 1, d), lambda i: (i // per_seq, 0, 0))],
        out_specs=pl.BlockSpec((tm, d), lambda i: (i, 0)),
        out_shape=jax.ShapeDtypeStruct((n, d), BF16),
        compiler_params=_params("arbitrary"),
        name="norm_modulate",
    )(x2d, w.reshape(1, d), sc.reshape(bsz, 1, d), sh.reshape(bsz, 1, d))


def _final_norm_kernel(x_ref, w_ref, o_ref):
    x = x_ref[...]
    inv = lax.rsqrt(jnp.mean(x * x, axis=-1, keepdims=True) + EPS)
    o_ref[...] = x * inv * w_ref[...]


def _final_norm(x2d, w, *, tm=256):
    n, d = x2d.shape
    return pl.pallas_call(
        _final_norm_kernel,
        grid=(n // tm,),
        in_specs=[pl.BlockSpec((tm, d), lambda i: (i, 0)),
                  pl.BlockSpec((1, d), lambda i: (0, 0))],
        out_specs=pl.BlockSpec((tm, d), lambda i: (i, 0)),
        out_shape=jax.ShapeDtypeStruct((n, d), F32),
        compiler_params=_params("arbitrary"),
        name="final_norm",
    )(x2d, w.reshape(1, d))


def _rope_table_kernel(pos_ref, invf_ref, sign_ref, cos_ref, sin_ref):
    ang = pos_ref[...].astype(F32) * invf_ref[...]
    cos_ref[...] = jnp.cos(ang)
    sin_ref[...] = jnp.sin(ang) * sign_ref[...]


def _rope_tables(positions, *, tm=2048):
    n = positions.size
    half = HEAD_DIM // 2
    inv_freq = ROPE_THETA ** (-jnp.arange(half, dtype=F32) / half)
    invf = jnp.concatenate([inv_freq, inv_freq]).reshape(1, HEAD_DIM)
    sign = jnp.concatenate([-jnp.ones((half,), F32), jnp.ones((half,), F32)]).reshape(1, HEAD_DIM)
    tm = min(tm, n)
    return pl.pallas_call(
        _rope_table_kernel,
        grid=(n // tm,),
        in_specs=[pl.BlockSpec((tm, 1), lambda i: (i, 0)),
                  pl.BlockSpec((1, HEAD_DIM), lambda i: (0, 0)),
                  pl.BlockSpec((1, HEAD_DIM), lambda i: (0, 0))],
        out_specs=[pl.BlockSpec((tm, HEAD_DIM), lambda i: (i, 0)),
                   pl.BlockSpec((tm, HEAD_DIM), lambda i: (i, 0))],
        out_shape=[jax.ShapeDtypeStruct((n, HEAD_DIM), F32),
                   jax.ShapeDtypeStruct((n, HEAD_DIM), F32)],
        compiler_params=_params("arbitrary"),
        name="rope_tables",
    )(positions.reshape(n, 1), invf, sign)


def _inproj_kernel(tile_ref, rope_ref, a_ref, w_ref, cos_ref, sin_ref, o_ref, *, head_major):
    del tile_ref
    rope = rope_ref[pl.program_id(1)] == 1
    acc = jnp.dot(a_ref[...], w_ref[...], preferred_element_type=F32)
    cos = cos_ref[...]
    sin = sin_ref[...]
    for h in range(acc.shape[1] // HEAD_DIM):
        xh = acc[:, h * HEAD_DIM:(h + 1) * HEAD_DIM]
        val = jnp.where(rope, xh * cos + pltpu.roll(xh, HEAD_DIM // 2, 1) * sin, xh)
        if head_major:
            o_ref[h] = val.astype(o_ref.dtype)
        else:
            o_ref[:, h * HEAD_DIM:(h + 1) * HEAD_DIM] = val.astype(o_ref.dtype)


def _in_projection(h, w, cos, sin, col_tiles, rope_flags, *, head_major, bm=1024, bn=1024):
    m, k = h.shape
    nt = len(col_tiles)
    if head_major:
        out_spec = pl.BlockSpec((bn // HEAD_DIM, bm, HEAD_DIM), lambda i, j, t, r: (j, i, 0))
        out_shape = jax.ShapeDtypeStruct((nt * (bn // HEAD_DIM), m, HEAD_DIM), F32)
    else:
        out_spec = pl.BlockSpec((bm, bn), lambda i, j, t, r: (i, j))
        out_shape = jax.ShapeDtypeStruct((m, nt * bn), BF16)
    return pl.pallas_call(
        functools.partial(_inproj_kernel, head_major=head_major),
        grid_spec=pltpu.PrefetchScalarGridSpec(
            num_scalar_prefetch=2,
            grid=(m // bm, nt),
            in_specs=[pl.BlockSpec((bm, k), lambda i, j, t, r: (i, 0)),
                      pl.BlockSpec((k, bn), lambda i, j, t, r: (0, t[j])),
                      pl.BlockSpec((bm, HEAD_DIM), lambda i, j, t, r: (i, 0)),
                      pl.BlockSpec((bm, HEAD_DIM), lambda i, j, t, r: (i, 0))],
            out_specs=out_spec),
        out_shape=out_shape,
        compiler_params=_params("arbitrary", "arbitrary"),
        name="in_projection_heads" if head_major else "in_projection",
    )(jnp.asarray(col_tiles, jnp.int32), jnp.asarray(rope_flags, jnp.int32), h, w, cos, sin)


def _merge_kernel(ya_ref, wa_ref, yb_ref, wb_ref, ga_ref, gb_ref, o_ref):
    a = jnp.dot(ya_ref[...], wa_ref[...], preferred_element_type=F32)
    b = jnp.dot(yb_ref[...], wb_ref[...], preferred_element_type=F32)
    ga = ga_ref[...].astype(F32)
    gb = gb_ref[...].astype(F32)
    o_ref[...] = (a / (1.0 + jnp.exp(-ga)) + b / (1.0 + jnp.exp(-gb))).astype(o_ref.dtype)


def _gated_merge(ya, wa, yb, wb, proj, ga_col, gb_col, *, bm=1024, bn=1024):
    m = ya.shape[0]
    n = wa.shape[1]
    ga_blk, gb_blk = ga_col // bn, gb_col // bn
    return pl.pallas_call(
        _merge_kernel,
        grid=(m // bm, n // bn),
        in_specs=[pl.BlockSpec((bm, ya.shape[1]), lambda i, j: (i, 0)),
                  pl.BlockSpec((wa.shape[0], bn), lambda i, j: (0, j)),
                  pl.BlockSpec((bm, yb.shape[1]), lambda i, j: (i, 0)),
                  pl.BlockSpec((wb.shape[0], bn), lambda i, j: (0, j)),
                  pl.BlockSpec((bm, bn), lambda i, j: (i, ga_blk + j)),
                  pl.BlockSpec((bm, bn), lambda i, j: (i, gb_blk + j))],
        out_specs=pl.BlockSpec((bm, bn), lambda i, j: (i, j)),
        out_shape=jax.ShapeDtypeStruct((m, n), BF16),
        compiler_params=_params("arbitrary", "arbitrary"),
        name="gated_merge",
    )(ya, wa, yb, wb, proj, proj)


def _residual_mm_kernel(a_ref, w_ref, x_ref, g_ref, o_ref):
    acc = jnp.dot(a_ref[...], w_ref[...], preferred_element_type=F32)
    o_ref[...] = x_ref[...] + g_ref[...] * acc


def _residual_matmul(a, w, x2d, gate, seq, *, bm=1024, bn=1024):
    m, k = a.shape
    n = w.shape[1]
    bsz = gate.shape[0]
    per_seq = seq // bm
    return pl.pallas_call(
        _residual_mm_kernel,
        grid=(m // bm, n // bn),
        in_specs=[pl.BlockSpec((bm, k), lambda i, j: (i, 0)),
                  pl.BlockSpec((k, bn), lambda i, j: (0, j)),
                  pl.BlockSpec((bm, bn), lambda i, j: (i, j)),
                  pl.BlockSpec((None, 1, bn), lambda i, j: (i // per_seq, 0, j))],
        out_specs=pl.BlockSpec((bm, bn), lambda i, j: (i, j)),
        out_shape=jax.ShapeDtypeStruct((m, n), F32),
        compiler_params=_params("arbitrary", "arbitrary"),
        name="out_projection",
    )(a, w, x2d, gate.reshape(bsz, 1, n))


def _relu2_mm_kernel(a_ref, w_ref, o_ref):
    acc = jnp.dot(a_ref[...], w_ref[...], preferred_element_type=F32)
    r = jnp.maximum(acc, 0.0)
    o_ref[...] = (r * r).astype(o_ref.dtype)


def _relu2_matmul(a, w, *, bm=1024, bn=1024):
    m, k = a.shape
    n = w.shape[1]
    return pl.pallas_call(
        _relu2_mm_kernel,
        grid=(m // bm, n // bn),
        in_specs=[pl.BlockSpec((bm, k), lambda i, j: (i, 0)),
                  pl.BlockSpec((k, bn), lambda i, j: (0, j))],
        out_specs=pl.BlockSpec((bm, bn), lambda i, j: (i, j)),
        out_shape=jax.ShapeDtypeStruct((m, n), BF16),
        compiler_params=_params("arbitrary", "arbitrary"),
        name="mlp_up",
    )(a, w)


def _residual_mm_ksplit_kernel(a_ref, w_ref, x_ref, g_ref, o_ref):
    kk = pl.program_id(2)

    @pl.when(kk == 0)
    def _():
        o_ref[...] = jnp.zeros_like(o_ref)

    o_ref[...] += jnp.dot(a_ref[...], w_ref[...], preferred_element_type=F32)

    @pl.when(kk == pl.num_programs(2) - 1)
    def _():
        o_ref[...] = x_ref[...] + g_ref[...] * o_ref[...]


def _residual_matmul_ksplit(a, w, x2d, gate, seq, *, bm=1024, bn=1024, bk=4096):
    m, k = a.shape
    n = w.shape[1]
    bsz = gate.shape[0]
    per_seq = seq // bm
    return pl.pallas_call(
        _residual_mm_ksplit_kernel,
        grid=(m // bm, n // bn, k // bk),
        in_specs=[pl.BlockSpec((bm, bk), lambda i, j, kk: (i, kk)),
                  pl.BlockSpec((bk, bn), lambda i, j, kk: (kk, j)),
                  pl.BlockSpec((bm, bn), lambda i, j, kk: (i, j)),
                  pl.BlockSpec((None, 1, bn), lambda i, j, kk: (i // per_seq, 0, j))],
        out_specs=pl.BlockSpec((bm, bn), lambda i, j, kk: (i, j)),
        out_shape=jax.ShapeDtypeStruct((m, n), F32),
        compiler_params=_params("arbitrary", "arbitrary", "arbitrary"),
        name="mlp_down",
    )(a, w, x2d, gate.reshape(bsz, 1, n))


def _suffix_sum_matrix(tk):
    s = np.arange(tk)[:, None]
    j = np.arange(2 * tk)[None, :] % tk
    return jnp.asarray(j >= s, dtype=BF16)


def _sb_kernel(q_ref, k_ref, v_ref, lt_ref, o_ref, vt_ref, acc_ref, carry_ref, *, tq, tk, scale):
    seq = q_ref.shape[0]
    n_heads = q_ref.shape[1] // HEAD_DIM
    n_diag = tq // tk
    z_scale = scale * LOG2_E
    sign_bit = jnp.uint32(0x80000000)

    def transpose_values(i, _):
        rows = pl.ds(pl.multiple_of(i * tq, tq), tq)
        for hd in range(n_heads):
            vt_ref[hd, :, rows] = v_ref[rows, hd * HEAD_DIM:(hd + 1) * HEAD_DIM].astype(F32).T.astype(BF16)
        return 0

    lax.fori_loop(0, seq // tq, transpose_values, 0)

    def visit(q0, c0, nc, k0, triangle):
        qcols = slice(c0, c0 + nc)
        heads = [slice(hd * HEAD_DIM, (hd + 1) * HEAD_DIM) for hd in range(n_heads)]
        if triangle:
            key = lax.broadcasted_iota(jnp.int32, (tk, nc), 0)
            qry = lax.broadcasted_iota(jnp.int32, (tk, nc), 1)
            causal = key < qry
        z2s = []
        for hcols in heads:
            q = q_ref[pl.ds(pl.multiple_of(q0 + c0, tk), nc), hcols]
            k = k_ref[pl.ds(k0, tk), hcols]
            z2s.append(lax.dot_general(k, q, (((1,), (1,)), ((), ())), preferred_element_type=F32) * z_scale)
        suffixes = []
        for z2 in z2s:
            neg_abs = lax.bitcast_convert_type(lax.bitcast_convert_type(z2, jnp.uint32) | sign_bit, F32)
            nlr = jnp.maximum(z2, 0.0) + jnp.log2(1.0 + jnp.exp2(neg_abs))
            if triangle:
                nlr = jnp.where(causal, nlr, 0.0)
            hi = nlr.astype(BF16)
            lo = (nlr - hi.astype(F32)).astype(BF16)
            suffixes.append(jnp.dot(lt_ref[...], jnp.concatenate([hi, lo], axis=0),
                                    preferred_element_type=F32))
        for hd in range(n_heads):
            carry = carry_ref[hd, :, qcols]
            a = jnp.exp2(z2s[hd] - suffixes[hd] - carry)
            if triangle:
                a = jnp.where(causal, a, 0.0)
            carry_ref[hd, :, qcols] = carry + suffixes[hd][0:1, :]
            acc_ref[hd, :, qcols] += jnp.dot(vt_ref[hd, :, pl.ds(k0, tk)], a.astype(BF16),
                                             preferred_element_type=F32)

    def q_block(iq, _):
        q0 = pl.multiple_of(iq * tq, tq)
        acc_ref[...] = jnp.zeros_like(acc_ref)
        carry_ref[...] = jnp.zeros_like(carry_ref)
        for d in reversed(range(n_diag)):
            visit(q0, d * tk, tk, q0 + d * tk, True)
            if d < n_diag - 1:
                visit(q0, (d + 1) * tk, tq - (d + 1) * tk, q0 + d * tk, False)

        def left(t, _):
            visit(q0, 0, tq, pl.multiple_of(q0 - (t + 1) * tk, tk), False)
            return 0

        lax.fori_loop(0, iq * n_diag, left, 0)
        for hd in range(n_heads):
            o_ref[pl.ds(q0, tq), hd * HEAD_DIM:(hd + 1) * HEAD_DIM] = acc_ref[hd].T.astype(o_ref.dtype)
        return 0

    lax.fori_loop(0, seq // tq, q_block, 0)


def _stick_breaking(proj, bsz, seq, n_heads, q_col, k_col, v_col, *, tq=512, tk=256, heads_per_step=4):
    hps = min(heads_per_step, n_heads)
    wb = hps * HEAD_DIM
    qb, kb, vb = q_col // wb, k_col // wb, v_col // wb
    kern = functools.partial(_sb_kernel, tq=tq, tk=tk, scale=HEAD_DIM ** -0.5)
    return pl.pallas_call(
        kern,
        grid=(bsz, n_heads // hps),
        in_specs=[pl.BlockSpec((seq, wb), lambda b, h: (b, qb + h)),
                  pl.BlockSpec((seq, wb), lambda b, h: (b, kb + h)),
                  pl.BlockSpec((seq, wb), lambda b, h: (b, vb + h)),
                  pl.BlockSpec((tk, 2 * tk), lambda b, h: (0, 0))],
        out_specs=pl.BlockSpec((seq, wb), lambda b, h: (b, h)),
        out_shape=jax.ShapeDtypeStruct((bsz * seq, n_heads * HEAD_DIM), BF16),
        scratch_shapes=[pltpu.VMEM((hps, HEAD_DIM, seq), BF16),
                        pltpu.VMEM((hps, HEAD_DIM, tq), F32),
                        pltpu.VMEM((hps, 1, tq), F32)],
        compiler_params=_params("arbitrary", "arbitrary"),
        name="stick_breaking",
    )(proj, proj, proj, _suffix_sum_matrix(tk))


WINDOW = 128
DIL_TILE = 1024
MIN_PROBLEM_ROWS = 256


def _window_softmax(q, k, v, valid, scale):
    s = lax.dot_general(q, k, (((1,), (1,)), ((), ())), preferred_element_type=F32) * scale
    s = jnp.where(valid, s, -jnp.inf)
    m = jnp.max(s, axis=1, keepdims=True)
    p = jnp.exp(s - m)
    l = jnp.sum(p, axis=1, keepdims=True)
    o = jnp.dot(p.astype(BF16), v, preferred_element_type=F32) / l
    return o, m + jnp.log(l)


def _dil_strided_kernel(q_ref, k_ref, v_ref, o_ref, lse_ref, qd_ref, kd_ref, vd_ref, *, dil, group, scale):
    n_heads, tile, _ = q_ref.shape
    chunk = tile // dil
    n_hist = -(-WINDOW // chunk)
    n_slots = n_hist + 1
    takes = [min(chunk, WINDOW - (back - 1) * chunk) for back in range(1, n_hist + 1)]
    assert group == 1 or all(t == chunk for t in takes)
    seg = group * chunk
    n_keys = group * (WINDOW + chunk)
    kt = pl.program_id(1)
    slot = lax.rem(kt, n_slots)

    @pl.when(kt == 0)
    def _():
        kd_ref[...] = jnp.zeros_like(kd_ref)
        vd_ref[...] = jnp.zeros_like(vd_ref)

    def deinterleave(c, _):
        rows = pl.ds(pl.multiple_of(c * chunk, chunk), chunk)
        for h in range(n_heads):
            qd_ref[h, rows, :] = q_ref[h, pl.ds(c, chunk, stride=dil), :].astype(BF16)
            kd_ref[slot, h, rows, :] = k_ref[h, pl.ds(c, chunk, stride=dil), :].astype(BF16)
            vd_ref[slot, h, rows, :] = v_ref[h, pl.ds(c, chunk, stride=dil), :].astype(BF16)
        return 0

    lax.fori_loop(0, dil, deinterleave, 0)

    row = lax.broadcasted_iota(jnp.int32, (seg, n_keys), 0)
    col = lax.broadcasted_iota(jnp.int32, (seg, n_keys), 1)
    if group == 1:
        i, u, same_class = row, col, None
    else:
        lc, ls = chunk.bit_length() - 1, seg.bit_length() - 1
        i = row & (chunk - 1)
        within = col & (seg - 1)
        u = ((col >> ls) << lc) + (within & (chunk - 1))
        same_class = (within >> lc) == (row >> lc)
    valid = jnp.logical_and(u >= i, u <= i + WINDOW)
    valid = jnp.logical_and(valid, u >= WINDOW - chunk * kt)
    if same_class is not None:
        valid = jnp.logical_and(valid, same_class)
    lane = lax.broadcasted_iota(jnp.int32, (seg, LANES), 1)

    def attend(gi, _):
        base = pl.multiple_of(gi * seg, seg)
        lse_blk = jnp.zeros((seg, LANES), F32)
        for h in range(n_heads):
            ks, vs = [], []
            for back in range(n_hist, 0, -1):
                take = takes[back - 1]
                old = lax.rem(kt + (n_slots - back), n_slots)
                rows = pl.ds(pl.multiple_of(base + (chunk - take), 16), take if group == 1 else seg)
                ks.append(kd_ref[old, h, rows, :])
                vs.append(vd_ref[old, h, rows, :])
            ks.append(kd_ref[slot, h, pl.ds(base, seg), :])
            vs.append(vd_ref[slot, h, pl.ds(base, seg), :])
            o, lse = _window_softmax(qd_ref[h, pl.ds(base, seg), :], jnp.concatenate(ks, axis=0),
                                     jnp.concatenate(vs, axis=0), valid, scale)
            for cc in range(group):
                o_ref[h, pl.ds(gi * group + cc, chunk, stride=dil), :] = o[cc * chunk:(cc + 1) * chunk]
            lse_blk = jnp.where(lane == h, lse, lse_blk)
        for cc in range(group):
            lse_ref[pl.ds(gi * group + cc, chunk, stride=dil), :] = lse_blk[cc * chunk:(cc + 1) * chunk]
        return 0

    lax.fori_loop(0, dil // group, attend, 0)


def _dil_strided(heads_qkv, g, bsz, seq, n_heads):
    n = heads_qkv.shape[1]
    dil = DIL_GROUPS[g][1]
    tile = min(DIL_TILE, seq)
    chunk = tile // dil
    n_slots = -(-WINDOW // chunk) + 1
    group = max(1, min(dil, MIN_PROBLEM_ROWS // chunk))
    per_seq = seq // tile

    def operand(which):
        return pl.BlockSpec((n_heads, tile, HEAD_DIM), lambda b, t: (2 * which + g - 1, b * per_seq + t, 0))

    return pl.pallas_call(
        functools.partial(_dil_strided_kernel, dil=dil, group=group, scale=HEAD_DIM ** -0.5),
        grid=(bsz, per_seq),
        in_specs=[operand(0), operand(1), operand(2)],
        out_specs=[pl.BlockSpec((n_heads, tile, HEAD_DIM), lambda b, t: (0, b * per_seq + t, 0)),
                   pl.BlockSpec((tile, LANES), lambda b, t: (b * per_seq + t, 0))],
        out_shape=[jax.ShapeDtypeStruct((n_heads, n, HEAD_DIM), F32),
                   jax.ShapeDtypeStruct((n, LANES), F32)],
        scratch_shapes=[pltpu.VMEM((n_heads, tile, HEAD_DIM), BF16),
                        pltpu.VMEM((n_slots, n_heads, tile, HEAD_DIM), BF16),
                        pltpu.VMEM((n_slots, n_heads, tile, HEAD_DIM), BF16)],
        compiler_params=_params("arbitrary", "arbitrary"),
        name=f"dilated_stride{dil}",
    )(heads_qkv, heads_qkv, heads_qkv)


def _dil_merge_kernel(q_ref, kp_ref, kc_ref, vp_ref, vc_ref, o2_ref, l2_ref, o3_ref, l3_ref, y_ref,
                      *, scale):
    tile = q_ref.shape[0]
    n_heads = q_ref.shape[1] // HEAD_DIM
    blk = WINDOW
    first = pl.program_id(1) == 0
    row = lax.broadcasted_iota(jnp.int32, (blk, 2 * blk), 0)
    col = lax.broadcasted_iota(jnp.int32, (blk, 2 * blk), 1)
    band = jnp.logical_and(col >= row, col <= row + blk)
    band_first = jnp.logical_and(band, jnp.logical_or(col >= blk, jnp.logical_not(first)))
    for sub in range(tile // blk):
        rows = slice(sub * blk, (sub + 1) * blk)
        for h in range(n_heads):
            cols = slice(h * HEAD_DIM, (h + 1) * HEAD_DIM)
            if sub == 0:
                k = jnp.concatenate([kp_ref[:, cols], kc_ref[rows, cols]], axis=0)
                v = jnp.concatenate([vp_ref[:, cols], vc_ref[rows, cols]], axis=0)
                valid = band_first
            else:
                k = kc_ref[(sub - 1) * blk:(sub + 1) * blk, cols]
                v = vc_ref[(sub - 1) * blk:(sub + 1) * blk, cols]
                valid = band
            o1, lse1 = _window_softmax(q_ref[rows, cols], k, v, valid, scale)
            lse2 = l2_ref[rows, h:h + 1]
            lse3 = l3_ref[rows, h:h + 1]
            mx = jnp.maximum(jnp.maximum(lse1, lse2), lse3)
            w1, w2, w3 = jnp.exp(lse1 - mx), jnp.exp(lse2 - mx), jnp.exp(lse3 - mx)
            y = (w1 * o1 + w2 * o2_ref[h, rows, :] + w3 * o3_ref[h, rows, :]) / (w1 + w2 + w3)
            y_ref[rows, cols] = y.astype(y_ref.dtype)


def _dil_merge(proj, q_col, k_col, v_col, o2, l2, o3, l3, bsz, seq, n_heads, *, tile=512):
    n = proj.shape[0]
    gw = n_heads * HEAD_DIM
    tile = min(tile, seq)
    per_seq = seq // tile
    sub_per_tile = tile // WINDOW
    sub_per_seq = seq // WINDOW
    qo, ko, vo = q_col // gw, k_col // gw, v_col // gw

    def cur(off):
        return pl.BlockSpec((tile, gw), lambda b, t: (b * per_seq + t, off))

    def prev(off):
        return pl.BlockSpec(
            (WINDOW, gw), lambda b, t: (b * sub_per_seq + jnp.maximum(t * sub_per_tile - 1, 0), off))

    heads = pl.BlockSpec((n_heads, tile, HEAD_DIM), lambda b, t: (0, b * per_seq + t, 0))
    lses = pl.BlockSpec((tile, LANES), lambda b, t: (b * per_seq + t, 0))
    return pl.pallas_call(
        functools.partial(_dil_merge_kernel, scale=HEAD_DIM ** -0.5),
        grid=(bsz, per_seq),
        in_specs=[cur(qo), prev(ko), cur(ko), prev(vo), cur(vo), heads, lses, heads, lses],
        out_specs=pl.BlockSpec((tile, gw), lambda b, t: (b * per_seq + t, 0)),
        out_shape=jax.ShapeDtypeStruct((n, gw), BF16),
        compiler_params=_params("arbitrary", "arbitrary"),
        name="dilated_merge",
    )(proj, proj, proj, proj, proj, o2, l2, o3, l3)


def _dilated_attention(proj, q_col, k_col, v_col, heads_qkv, bsz, seq, n_heads):
    o3, l3 = _dil_strided(heads_qkv, 2, bsz, seq, n_heads)
    o2, l2 = _dil_strided(heads_qkv, 1, bsz, seq, n_heads)
    return _dil_merge(proj, q_col, k_col, v_col, o2, l2, o3, l3, bsz, seq, n_heads)


def kernel(x, c, positions, ada_w, ada_b, norm_mix_w, w_in, w_o_sb, w_o_dil, w_out,
           norm_mlp_w, w_ff1, w_ff2, norm_final_w):
    bsz, seq, d = x.shape
    n = bsz * seq
    d_sb = w_o_sb.shape[1]
    d_dil = w_o_dil.shape[1]
    sb_heads = d_sb // HEAD_DIM
    dil_heads = d_dil // HEAD_DIM
    bn = d_dil
    sb_t = 3 * d_sb // bn
    dil_t = lambda which, g: sb_t + which * N_DIL + g
    gate_t = sb_t + 3 * N_DIL
    main_tiles = (list(range(sb_t)) + [dil_t(0, 0), dil_t(1, 0), dil_t(2, 0)]
                  + list(range(gate_t, gate_t + 2 * d // bn)))
    main_rope = [0] * sb_t + [1, 1, 0] + [0] * (2 * d // bn)
    q0_col, k0_col, v0_col = sb_t * bn, (sb_t + 1) * bn, (sb_t + 2) * bn
    g_sb_col = (sb_t + 3) * bn
    g_dil_col = g_sb_col + d
    head_tiles = [dil_t(which, g) for which in range(3) for g in (1, 2)]
    head_rope = [1, 1, 1, 1, 0, 0]

    cos, sin = _rope_tables(positions)
    xf = x.reshape(n, d)
    for l in range(ada_w.shape[0]):
        mod = _ada_modulation(c, ada_w[l], ada_b[l])
        sh1, sc1, g1, sh2, sc2, g2 = [mod[:, i * d:(i + 1) * d] for i in range(6)]

        h = _norm_mod(xf, norm_mix_w[l], sc1, sh1, seq)
        w_in_l = w_in[l].astype(BF16)
        proj = _in_projection(h, w_in_l, cos, sin, main_tiles, main_rope, head_major=False, bn=bn)
        heads_qkv = _in_projection(h, w_in_l, cos, sin, head_tiles, head_rope, head_major=True, bn=bn)
        y_a = _stick_breaking(proj, bsz, seq, sb_heads, 0, d_sb, 2 * d_sb)
        y_b = _dilated_attention(proj, q0_col, k0_col, v0_col, heads_qkv, bsz, seq, dil_heads)
        merged = _gated_merge(y_a, w_o_sb[l].astype(BF16), y_b, w_o_dil[l].astype(BF16),
                              proj, g_sb_col, g_dil_col)
        xf = _residual_matmul(merged, w_out[l].astype(BF16), xf, g1, seq)

        h = _norm_mod(xf, norm_mlp_w[l], sc2, sh2, seq)
        u = _relu2_matmul(h, w_ff1[l].astype(BF16))
        xf = _residual_matmul_ksplit(u, w_ff2[l].astype(BF16), xf, g2, seq)
    return _final_norm(xf, norm_final_w).reshape(bsz, seq, d)
```

```python
import functools

import jax
import jax.numpy as jnp
import numpy as np
from jax import lax
from jax.experimental import pallas as pl
from jax.experimental.pallas import tpu as pltpu

F32 = jnp.float32
BF16 = jnp.bfloat16

HEAD_DIM = 128
LANES = 128
DIL_GROUPS = ((128, 1), (512, 4), (2048, 16))
N_DIL = len(DIL_GROUPS)
ROPE_THETA = 10000.0
EPS = 1e-6
LOG2_E = 1.4426950408889634
SATURATED = 160.0
VMEM_LIMIT = 56 * 1024 * 1024


def _params(*sem):
    return pltpu.CompilerParams(dimension_semantics=sem, vmem_limit_bytes=VMEM_LIMIT)


def _ada_kernel(c_ref, w_ref, b_ref, o_ref):
    c = c_ref[...]
    s = c * (1.0 / (1.0 + jnp.exp(-c)))
    o_ref[...] = jnp.dot(s.astype(BF16), w_ref[...].astype(BF16),
                         preferred_element_type=F32) + b_ref[...]


def _ada_modulation(c, w, b, *, bn=512):
    bsz, d = c.shape
    n = w.shape[1]
    rows = 8
    c_pad = jnp.zeros((rows, d), F32).at[:bsz].set(c)
    out = pl.pallas_call(
        _ada_kernel,
        grid=(n // bn,),
        in_specs=[pl.BlockSpec((rows, d), lambda j: (0, 0)),
                  pl.BlockSpec((d, bn), lambda j: (0, j)),
                  pl.BlockSpec((1, bn), lambda j: (0, j))],
        out_specs=pl.BlockSpec((rows, bn), lambda j: (0, j)),
        out_shape=jax.ShapeDtypeStruct((rows, n), F32),
        compiler_params=_params("arbitrary"),
        name="ada_modulation",
    )(c_pad, w, b.reshape(1, n))
    return out[:bsz]


def _norm_mod_kernel(x_ref, w_ref, sc_ref, sh_ref, o_ref):
    x = x_ref[...]
    inv = lax.rsqrt(jnp.mean(x * x, axis=-1, keepdims=True) + EPS)
    mul = w_ref[...] * (1.0 + sc_ref[...])
    o_ref[...] = (x * inv * mul + sh_ref[...]).astype(o_ref.dtype)


def _norm_mod(x2d, w, sc, sh, seq, *, tm=512):
    n, d = x2d.shape
    bsz = sc.shape[0]
    per_seq = seq // tm
    return pl.pallas_call(
        _norm_mod_kernel,
        grid=(n // tm,),
        in_specs=[pl.BlockSpec((tm, d), lambda i: (i, 0)),
                  pl.BlockSpec((1, d), lambda i: (0, 0)),
                  pl.BlockSpec((None, 1, d), lambda i: (i // per_seq, 0, 0)),
                  pl.BlockSpec((None, 1, d), lambda i: (i // per_seq, 0, 0))],
        out_specs=pl.BlockSpec((tm, d), lambda i: (i, 0)),
        out_shape=jax.ShapeDtypeStruct((n, d), BF16),
        compiler_params=_params("arbitrary"),
        name="norm_modulate",
    )(x2d, w.reshape(1, d), sc.reshape(bsz, 1, d), sh.reshape(bsz, 1, d))


def _final_norm_kernel(x_ref, w_ref, o_ref):
    x = x_ref[...]
    inv = lax.rsqrt(jnp.mean(x * x, axis=-1, keepdims=True) + EPS)
    o_ref[...] = x * inv * w_ref[...]


def _final_norm(x2d, w, *, tm=512):
    n, d = x2d.shape
    return pl.pallas_call(
        _final_norm_kernel,
        grid=(n // tm,),
        in_specs=[pl.BlockSpec((tm, d), lambda i: (i, 0)),
                  pl.BlockSpec((1, d), lambda i: (0, 0))],
        out_specs=pl.BlockSpec((tm, d), lambda i: (i, 0)),
        out_shape=jax.ShapeDtypeStruct((n, d), F32),
        compiler_params=_params("arbitrary"),
        name="final_norm",
    )(x2d, w.reshape(1, d))


def _rope_table_kernel(pos_ref, invf_ref, sign_ref, cos_ref, sin_ref):
    ang = pos_ref[...].astype(F32) * invf_ref[...]
    cos_ref[...] = jnp.cos(ang)
    sin_ref[...] = jnp.sin(ang) * sign_ref[...]


def _rope_tables(positions, *, tm=2048):
    n = positions.size
    half = HEAD_DIM // 2
    inv_freq = ROPE_THETA ** (-jnp.arange(half, dtype=F32) / half)
    invf = jnp.concatenate([inv_freq, inv_freq]).reshape(1, HEAD_DIM)
    sign = jnp.concatenate([-jnp.ones((half,), F32), jnp.ones((half,), F32)]).reshape(1, HEAD_DIM)
    tm = min(tm, n)
    return pl.pallas_call(
        _rope_table_kernel,
        grid=(n // tm,),
        in_specs=[pl.BlockSpec((tm, 1), lambda i: (i, 0)),
                  pl.BlockSpec((1, HEAD_DIM), lambda i: (0, 0)),
                  pl.BlockSpec((1, HEAD_DIM), lambda i: (0, 0))],
        out_specs=[pl.BlockSpec((tm, HEAD_DIM), lambda i: (i, 0)),
                   pl.BlockSpec((tm, HEAD_DIM), lambda i: (i, 0))],
        out_shape=[jax.ShapeDtypeStruct((n, HEAD_DIM), F32),
                   jax.ShapeDtypeStruct((n, HEAD_DIM), F32)],
        compiler_params=_params("arbitrary"),
        name="rope_tables",
    )(positions.reshape(n, 1), invf, sign)


def _inproj_kernel(tile_ref, rope_ref, a_ref, w_ref, cos_ref, sin_ref, o_ref, *, head_major):
    del tile_ref
    rope = rope_ref[pl.program_id(1)] == 1
    acc = jnp.dot(a_ref[...], w_ref[...], preferred_element_type=F32)
    cos = cos_ref[...]
    sin = sin_ref[...]
    for h in range(acc.shape[1] // HEAD_DIM):
        xh = acc[:, h * HEAD_DIM:(h + 1) * HEAD_DIM]
        val = jnp.where(rope, xh * cos + pltpu.roll(xh, HEAD_DIM // 2, 1) * sin, xh)
        if head_major:
            o_ref[h] = val.astype(o_ref.dtype)
        else:
            o_ref[:, h * HEAD_DIM:(h + 1) * HEAD_DIM] = val.astype(o_ref.dtype)


def _in_projection(h, w, cos, sin, col_tiles, rope_flags, *, head_major, bm=1024, bn=1024):
    m, k = h.shape
    nt = len(col_tiles)
    if head_major:
        out_spec = pl.BlockSpec((bn // HEAD_DIM, bm, HEAD_DIM), lambda i, j, t, r: (j, i, 0))
        out_shape = jax.ShapeDtypeStruct((nt * (bn // HEAD_DIM), m, HEAD_DIM), F32)
    else:
        out_spec = pl.BlockSpec((bm, bn), lambda i, j, t, r: (i, j))
        out_shape = jax.ShapeDtypeStruct((m, nt * bn), BF16)
    return pl.pallas_call(
        functools.partial(_inproj_kernel, head_major=head_major),
        grid_spec=pltpu.PrefetchScalarGridSpec(
            num_scalar_prefetch=2,
            grid=(m // bm, nt),
            in_specs=[pl.BlockSpec((bm, k), lambda i, j, t, r: (i, 0)),
                      pl.BlockSpec((k, bn), lambda i, j, t, r: (0, t[j])),
                      pl.BlockSpec((bm, HEAD_DIM), lambda i, j, t, r: (i, 0)),
                      pl.BlockSpec((bm, HEAD_DIM), lambda i, j, t, r: (i, 0))],
            out_specs=out_spec),
        out_shape=out_shape,
        compiler_params=_params("arbitrary", "arbitrary"),
        name="in_projection_heads" if head_major else "in_projection",
    )(jnp.asarray(col_tiles, jnp.int32), jnp.asarray(rope_flags, jnp.int32), h, w, cos, sin)


def _merge_kernel(ya_ref, wa_ref, yb_ref, wb_ref, ga_ref, gb_ref, o_ref):
    a = jnp.dot(ya_ref[...], wa_ref[...], preferred_element_type=F32)
    b = jnp.dot(yb_ref[...], wb_ref[...], preferred_element_type=F32)
    ga = ga_ref[...].astype(F32)
    gb = gb_ref[...].astype(F32)
    o_ref[...] = (a / (1.0 + jnp.exp(-ga)) + b / (1.0 + jnp.exp(-gb))).astype(o_ref.dtype)


def _gated_merge(ya, wa, yb, wb, proj, ga_col, gb_col, *, bm=1024, bn=1024):
    m = ya.shape[0]
    n = wa.shape[1]
    ga_blk, gb_blk = ga_col // bn, gb_col // bn
    return pl.pallas_call(
        _merge_kernel,
        grid=(m // bm, n // bn),
        in_specs=[pl.BlockSpec((bm, ya.shape[1]), lambda i, j: (i, 0)),
                  pl.BlockSpec((wa.shape[0], bn), lambda i, j: (0, j)),
                  pl.BlockSpec((bm, yb.shape[1]), lambda i, j: (i, 0)),
                  pl.BlockSpec((wb.shape[0], bn), lambda i, j: (0, j)),
                  pl.BlockSpec((bm, bn), lambda i, j: (i, ga_blk + j)),
                  pl.BlockSpec((bm, bn), lambda i, j: (i, gb_blk + j))],
        out_specs=pl.BlockSpec((bm, bn), lambda i, j: (i, j)),
        out_shape=jax.ShapeDtypeStruct((m, n), BF16),
        compiler_params=_params("arbitrary", "arbitrary"),
        name="gated_merge",
    )(ya, wa, yb, wb, proj, proj)


def _residual_mm_kernel(a_ref, w_ref, x_ref, g_ref, o_ref):
    acc = jnp.dot(a_ref[...], w_ref[...], preferred_element_type=F32)
    o_ref[...] = x_ref[...] + g_ref[...] * acc


def _residual_matmul(a, w, x2d, gate, seq, *, bm=1024, bn=1024):
    m, k = a.shape
    n = w.shape[1]
    bsz = gate.shape[0]
    per_seq = seq // bm
    return pl.pallas_call(
        _residual_mm_kernel,
        grid=(m // bm, n // bn),
        in_specs=[pl.BlockSpec((bm, k), lambda i, j: (i, 0)),
                  pl.BlockSpec((k, bn), lambda i, j: (0, j)),
                  pl.BlockSpec((bm, bn), lambda i, j: (i, j)),
                  pl.BlockSpec((None, 1, bn), lambda i, j: (i // per_seq, 0, j))],
        out_specs=pl.BlockSpec((bm, bn), lambda i, j: (i, j)),
        out_shape=jax.ShapeDtypeStruct((m, n), F32),
        compiler_params=_params("arbitrary", "arbitrary"),
        name="out_projection",
    )(a, w, x2d, gate.reshape(bsz, 1, n))


def _relu2_mm_kernel(a_ref, w_ref, o_ref):
    acc = jnp.dot(a_ref[...], w_ref[...], preferred_element_type=F32)
    r = jnp.maximum(acc, 0.0)
    o_ref[...] = (r * r).astype(o_ref.dtype)


def _relu2_matmul(a, w, *, bm=1024, bn=1024):
    m, k = a.shape
    n = w.shape[1]
    return pl.pallas_call(
        _relu2_mm_kernel,
        grid=(m // bm, n // bn),
        in_specs=[pl.BlockSpec((bm, k), lambda i, j: (i, 0)),
                  pl.BlockSpec((k, bn), lambda i, j: (0, j))],
        out_specs=pl.BlockSpec((bm, bn), lambda i, j: (i, j)),
        out_shape=jax.ShapeDtypeStruct((m, n), BF16),
        compiler_params=_params("arbitrary", "arbitrary"),
        name="mlp_up",
    )(a, w)


def _residual_mm_ksplit_kernel(a_ref, w_ref, x_ref, g_ref, o_ref):
    kk = pl.program_id(2)

    @pl.when(kk == 0)
    def _():
        o_ref[...] = jnp.zeros_like(o_ref)

    o_ref[...] += jnp.dot(a_ref[...], w_ref[...], preferred_element_type=F32)

    @pl.when(kk == pl.num_programs(2) - 1)
    def _():
        o_ref[...] = x_ref[...] + g_ref[...] * o_ref[...]


def _residual_matmul_ksplit(a, w, x2d, gate, seq, *, bm=1024, bn=1024, bk=4096):
    m, k = a.shape
    n = w.shape[1]
    bsz = gate.shape[0]
    per_seq = seq // bm
    return pl.pallas_call(
        _residual_mm_ksplit_kernel,
        grid=(m // bm, n // bn, k // bk),
        in_specs=[pl.BlockSpec((bm, bk), lambda i, j, kk: (i, kk)),
                  pl.BlockSpec((bk, bn), lambda i, j, kk: (kk, j)),
                  pl.BlockSpec((bm, bn), lambda i, j, kk: (i, j)),
                  pl.BlockSpec((None, 1, bn), lambda i, j, kk: (i // per_seq, 0, j))],
        out_specs=pl.BlockSpec((bm, bn), lambda i, j, kk: (i, j)),
        out_shape=jax.ShapeDtypeStruct((m, n), F32),
        compiler_params=_params("arbitrary", "arbitrary", "arbitrary"),
        name="mlp_down",
    )(a, w, x2d, gate.reshape(bsz, 1, n))


def _suffix_sum_matrix():
    r = np.arange(2 * LANES)[:, None] % LANES
    c = np.arange(2 * LANES)[None, :]
    return jnp.asarray(np.where(c < LANES, r >= c, True), dtype=BF16)


def _sb_kernel(q_ref, k_ref, v_ref, ut_ref, o_ref, acc_ref, carry_ref, *, tq, tk, scale):
    seq = q_ref.shape[0]
    n_heads = q_ref.shape[1] // HEAD_DIM
    n_diag = tq // tk
    z_scale = scale * LOG2_E

    def visit(hd, q0, r0, nr, k0, triangle):
        hcols = slice(hd * HEAD_DIM, (hd + 1) * HEAD_DIM)
        rows = slice(r0, r0 + nr)
        q = q_ref[pl.ds(pl.multiple_of(q0 + r0, tk), nr), hcols]
        k = k_ref[pl.ds(k0, tk), hcols]
        v = v_ref[pl.ds(k0, tk), hcols]
        z2 = lax.dot_general(q, k, (((1,), (1,)), ((), ())), preferred_element_type=F32) * z_scale
        nlr = jnp.maximum(z2, 0.0) + jnp.log2(1.0 + jnp.exp2(-jnp.abs(z2)))
        if triangle:
            row = lax.broadcasted_iota(jnp.int32, (nr, tk), 0)
            col = lax.broadcasted_iota(jnp.int32, (nr, tk), 1)
            causal = col < row
            nlr = jnp.where(causal, nlr, 0.0)
        carry = carry_ref[hd, rows, :]
        parts = [None] * (tk // LANES)
        for sb in reversed(range(tk // LANES)):
            cols = slice(sb * LANES, (sb + 1) * LANES)
            x = nlr[:, cols]
            hi = x.astype(BF16)
            lo = (x - hi.astype(F32)).astype(BF16)
            cs = jnp.dot(jnp.concatenate([hi, lo], axis=1), ut_ref[...], preferred_element_type=F32)
            a = jnp.exp2(z2[:, cols] - cs[:, :LANES] - carry)
            if triangle:
                a = jnp.where(causal[:, cols], a, 0.0)
            parts[sb] = a.astype(BF16)
            carry = carry + cs[:, LANES:]
        carry_ref[hd, rows, :] = carry
        acc_ref[hd, rows, :] += jnp.dot(jnp.concatenate(parts, axis=1), v, preferred_element_type=F32)

    def q_block(iq, _):
        q0 = pl.multiple_of(iq * tq, tq)
        acc_ref[...] = jnp.zeros_like(acc_ref)
        carry_ref[...] = jnp.zeros_like(carry_ref)
        for d in reversed(range(n_diag)):
            for hd in range(n_heads):
                visit(hd, q0, d * tk, tk, q0 + d * tk, True)
                if d < n_diag - 1:
                    visit(hd, q0, (d + 1) * tk, tq - (d + 1) * tk, q0 + d * tk, False)

        def more(state):
            t, lowest = state
            return jnp.logical_and(t < iq * n_diag, lowest < SATURATED)

        def left(state):
            t, _ = state
            k0 = pl.multiple_of(q0 - (t + 1) * tk, tk)
            for hd in range(n_heads):
                visit(hd, q0, 0, tq, k0, False)
            return t + 1, jnp.min(carry_ref[...])

        lax.while_loop(more, left, (jnp.int32(0), jnp.min(carry_ref[...])))
        for hd in range(n_heads):
            o_ref[pl.ds(q0, tq), hd * HEAD_DIM:(hd + 1) * HEAD_DIM] = acc_ref[hd].astype(o_ref.dtype)
        return 0

    lax.fori_loop(0, seq // tq, q_block, 0)


def _stick_breaking(proj, bsz, seq, n_heads, q_col, k_col, v_col, *, tq=512, tk=256, heads_per_step=4):
    hps = min(heads_per_step, n_heads)
    wb = hps * HEAD_DIM
    qb, kb, vb = q_col // wb, k_col // wb, v_col // wb
    kern = functools.partial(_sb_kernel, tq=tq, tk=tk, scale=HEAD_DIM ** -0.5)
    return pl.pallas_call(
        kern,
        grid=(bsz, n_heads // hps),
        in_specs=[pl.BlockSpec((seq, wb), lambda b, h: (b, qb + h)),
                  pl.BlockSpec((seq, wb), lambda b, h: (b, kb + h)),
                  pl.BlockSpec((seq, wb), lambda b, h: (b, vb + h)),
                  pl.BlockSpec((2 * LANES, 2 * LANES), lambda b, h: (0, 0))],
        out_specs=pl.BlockSpec((seq, wb), lambda b, h: (b, h)),
        out_shape=jax.ShapeDtypeStruct((bsz * seq, n_heads * HEAD_DIM), BF16),
        scratch_shapes=[pltpu.VMEM((hps, tq, HEAD_DIM), F32), pltpu.VMEM((hps, tq, LANES), F32)],
        compiler_params=_params("arbitrary", "arbitrary"),
        name="stick_breaking",
    )(proj, proj, proj, _suffix_sum_matrix())


WINDOW = 128
DIL_TILE = 1024
MIN_PROBLEM_ROWS = 256


def _window_softmax(qs, ks, vs, valid, scale):
    scores = []
    for q, k in zip(qs, ks):
        s = lax.dot_general(q, k, (((1,), (1,)), ((), ())), preferred_element_type=F32) * scale
        scores.append(jnp.where(valid, s, -jnp.inf))
    stats = []
    for s in scores:
        m = jnp.max(s, axis=1, keepdims=True)
        p = jnp.exp(s - m)
        stats.append((p.astype(BF16), m, jnp.sum(p, axis=1, keepdims=True)))
    return [(jnp.dot(p, v, preferred_element_type=F32) / l, m + jnp.log(l))
            for (p, m, l), v in zip(stats, vs)]


def _dil_strided_kernel(q_ref, k_ref, v_ref, o_ref, lse_ref, qd_ref, kd_ref, vd_ref, *, dil, group, scale):
    n_heads, tile, _ = q_ref.shape
    chunk = tile // dil
    n_hist = -(-WINDOW // chunk)
    n_slots = n_hist + 1
    takes = [min(chunk, WINDOW - (back - 1) * chunk) for back in range(1, n_hist + 1)]
    assert group == 1 or all(t == chunk for t in takes)
    seg = group * chunk
    n_keys = group * (WINDOW + chunk)
    kt = pl.program_id(1)
    slot = lax.rem(kt, n_slots)

    @pl.when(kt == 0)
    def _():
        kd_ref[...] = jnp.zeros_like(kd_ref)
        vd_ref[...] = jnp.zeros_like(vd_ref)

    def deinterleave(c, _):
        rows = pl.ds(pl.multiple_of(c * chunk, chunk), chunk)
        for h in range(n_heads):
            qd_ref[h, rows, :] = q_ref[h, pl.ds(c, chunk, stride=dil), :].astype(BF16)
            kd_ref[slot, h, rows, :] = k_ref[h, pl.ds(c, chunk, stride=dil), :].astype(BF16)
            vd_ref[slot, h, rows, :] = v_ref[h, pl.ds(c, chunk, stride=dil), :].astype(BF16)
        return 0

    lax.fori_loop(0, dil, deinterleave, 0)

    row = lax.broadcasted_iota(jnp.int32, (seg, n_keys), 0)
    col = lax.broadcasted_iota(jnp.int32, (seg, n_keys), 1)
    if group == 1:
        i, u, same_class = row, col, None
    else:
        lc, ls = chunk.bit_length() - 1, seg.bit_length() - 1
        i = row & (chunk - 1)
        within = col & (seg - 1)
        u = ((col >> ls) << lc) + (within & (chunk - 1))
        same_class = (within >> lc) == (row >> lc)
    valid = jnp.logical_and(u >= i, u <= i + WINDOW)
    valid = jnp.logical_and(valid, u >= WINDOW - chunk * kt)
    if same_class is not None:
        valid = jnp.logical_and(valid, same_class)
    lane = lax.broadcasted_iota(jnp.int32, (seg, LANES), 1)

    def attend(gi, _):
        base = pl.multiple_of(gi * seg, seg)
        qs, ks, vs = [], [], []
        for h in range(n_heads):
            kparts, vparts = [], []
            for back in range(n_hist, 0, -1):
                take = takes[back - 1]
                old = lax.rem(kt + (n_slots - back), n_slots)
                rows = pl.ds(pl.multiple_of(base + (chunk - take), 16), take if group == 1 else seg)
                kparts.append(kd_ref[old, h, rows, :])
                vparts.append(vd_ref[old, h, rows, :])
            kparts.append(kd_ref[slot, h, pl.ds(base, seg), :])
            vparts.append(vd_ref[slot, h, pl.ds(base, seg), :])
            qs.append(qd_ref[h, pl.ds(base, seg), :])
            ks.append(jnp.concatenate(kparts, axis=0))
            vs.append(jnp.concatenate(vparts, axis=0))
        lse_blk = jnp.zeros((seg, LANES), F32)
        for h, (o, lse) in enumerate(_window_softmax(qs, ks, vs, valid, scale)):
            for cc in range(group):
                o_ref[h, pl.ds(gi * group + cc, chunk, stride=dil), :] = o[cc * chunk:(cc + 1) * chunk]
            lse_blk = jnp.where(lane == h, lse, lse_blk)
        for cc in range(group):
            lse_ref[pl.ds(gi * group + cc, chunk, stride=dil), :] = lse_blk[cc * chunk:(cc + 1) * chunk]
        return 0

    lax.fori_loop(0, dil // group, attend, 0)


def _dil_strided(heads_qkv, g, bsz, seq, n_heads):
    n = heads_qkv.shape[1]
    dil = DIL_GROUPS[g][1]
    tile = min(DIL_TILE, seq)
    chunk = tile // dil
    n_slots = -(-WINDOW // chunk) + 1
    group = max(1, min(dil, MIN_PROBLEM_ROWS // chunk))
    per_seq = seq // tile

    def operand(which):
        return pl.BlockSpec((n_heads, tile, HEAD_DIM), lambda b, t: (2 * which + g - 1, b * per_seq + t, 0))

    return pl.pallas_call(
        functools.partial(_dil_strided_kernel, dil=dil, group=group, scale=HEAD_DIM ** -0.5),
        grid=(bsz, per_seq),
        in_specs=[operand(0), operand(1), operand(2)],
        out_specs=[pl.BlockSpec((n_heads, tile, HEAD_DIM), lambda b, t: (0, b * per_seq + t, 0)),
                   pl.BlockSpec((tile, LANES), lambda b, t: (b * per_seq + t, 0))],
        out_shape=[jax.ShapeDtypeStruct((n_heads, n, HEAD_DIM), F32),
                   jax.ShapeDtypeStruct((n, LANES), F32)],
        scratch_shapes=[pltpu.VMEM((n_heads, tile, HEAD_DIM), BF16),
                        pltpu.VMEM((n_slots, n_heads, tile, HEAD_DIM), BF16),
                        pltpu.VMEM((n_slots, n_heads, tile, HEAD_DIM), BF16)],
        compiler_params=_params("arbitrary", "arbitrary"),
        name=f"dilated_stride{dil}",
    )(heads_qkv, heads_qkv, heads_qkv)


def _dil_merge_kernel(q_ref, kp_ref, kc_ref, vp_ref, vc_ref, o2_ref, l2_ref, o3_ref, l3_ref, y_ref,
                      *, scale):
    tile = q_ref.shape[0]
    n_heads = q_ref.shape[1] // HEAD_DIM
    blk = WINDOW
    first = pl.program_id(1) == 0
    row = lax.broadcasted_iota(jnp.int32, (blk, 2 * blk), 0)
    col = lax.broadcasted_iota(jnp.int32, (blk, 2 * blk), 1)
    band = jnp.logical_and(col >= row, col <= row + blk)
    band_first = jnp.logical_and(band, jnp.logical_or(col >= blk, jnp.logical_not(first)))
    for sub in range(tile // blk):
        rows = slice(sub * blk, (sub + 1) * blk)
        heads = [slice(h * HEAD_DIM, (h + 1) * HEAD_DIM) for h in range(n_heads)]
        if sub == 0:
            ks = [jnp.concatenate([kp_ref[:, cols], kc_ref[rows, cols]], axis=0) for cols in heads]
            vs = [jnp.concatenate([vp_ref[:, cols], vc_ref[rows, cols]], axis=0) for cols in heads]
            valid = band_first
        else:
            ks = [kc_ref[(sub - 1) * blk:(sub + 1) * blk, cols] for cols in heads]
            vs = [vc_ref[(sub - 1) * blk:(sub + 1) * blk, cols] for cols in heads]
            valid = band
        for h in range(n_heads):
            (o1, lse1), = _window_softmax([q_ref[rows, heads[h]]], ks[h:h + 1], vs[h:h + 1], valid, scale)
            lse2 = l2_ref[rows, h:h + 1]
            lse3 = l3_ref[rows, h:h + 1]
            mx = jnp.maximum(jnp.maximum(lse1, lse2), lse3)
            w1, w2, w3 = jnp.exp(lse1 - mx), jnp.exp(lse2 - mx), jnp.exp(lse3 - mx)
            y = (w1 * o1 + w2 * o2_ref[h, rows, :] + w3 * o3_ref[h, rows, :]) / (w1 + w2 + w3)
            y_ref[rows, heads[h]] = y.astype(y_ref.dtype)


def _dil_merge(proj, q_col, k_col, v_col, o2, l2, o3, l3, bsz, seq, n_heads, *, tile=512):
    n = proj.shape[0]
    gw = n_heads * HEAD_DIM
    tile = min(tile, seq)
    per_seq = seq // tile
    sub_per_tile = tile // WINDOW
    sub_per_seq = seq // WINDOW
    qo, ko, vo = q_col // gw, k_col // gw, v_col // gw

    def cur(off):
        return pl.BlockSpec((tile, gw), lambda b, t: (b * per_seq + t, off))

    def prev(off):
        return pl.BlockSpec(
            (WINDOW, gw), lambda b, t: (b * sub_per_seq + jnp.maximum(t * sub_per_tile - 1, 0), off))

    heads = pl.BlockSpec((n_heads, tile, HEAD_DIM), lambda b, t: (0, b * per_seq + t, 0))
    lses = pl.BlockSpec((tile, LANES), lambda b, t: (b * per_seq + t, 0))
    return pl.pallas_call(
        functools.partial(_dil_merge_kernel, scale=HEAD_DIM ** -0.5),
        grid=(bsz, per_seq),
        in_specs=[cur(qo), prev(ko), cur(ko), prev(vo), cur(vo), heads, lses, heads, lses],
        out_specs=pl.BlockSpec((tile, gw), lambda b, t: (b * per_seq + t, 0)),
        out_shape=jax.ShapeDtypeStruct((n, gw), BF16),
        compiler_params=_params("arbitrary", "arbitrary"),
        name="dilated_merge",
    )(proj, proj, proj, proj, proj, o2, l2, o3, l3)


def _dilated_attention(proj, q_col, k_col, v_col, heads_qkv, bsz, seq, n_heads):
    o3, l3 = _dil_strided(heads_qkv, 2, bsz, seq, n_heads)
    o2, l2 = _dil_strided(heads_qkv, 1, bsz, seq, n_heads)
    return _dil_merge(proj, q_col, k_col, v_col, o2, l2, o3, l3, bsz, seq, n_heads)


def kernel(x, c, positions, ada_w, ada_b, norm_mix_w, w_in, w_o_sb, w_o_dil, w_out,
           norm_mlp_w, w_ff1, w_ff2, norm_final_w):
    bsz, seq, d = x.shape
    n = bsz * seq
    d_sb = w_o_sb.shape[1]
    d_dil = w_o_dil.shape[1]
    sb_heads = d_sb // HEAD_DIM
    dil_heads = d_dil // HEAD_DIM
    bn = d_dil
    sb_t = 3 * d_sb // bn
    dil_t = lambda which, g: sb_t + which * N_DIL + g
    gate_t = sb_t + 3 * N_DIL
    main_tiles = (list(range(sb_t)) + [dil_t(0, 0), dil_t(1, 0), dil_t(2, 0)]
                  + list(range(gate_t, gate_t + 2 * d // bn)))
    main_rope = [0] * sb_t + [1, 1, 0] + [0] * (2 * d // bn)
    q0_col, k0_col, v0_col = sb_t * bn, (sb_t + 1) * bn, (sb_t + 2) * bn
    g_sb_col = (sb_t + 3) * bn
    g_dil_col = g_sb_col + d
    head_tiles = [dil_t(which, g) for which in range(3) for g in (1, 2)]
    head_rope = [1, 1, 1, 1, 0, 0]

    cos, sin = _rope_tables(positions)
    xf = x.reshape(n, d)
    for l in range(ada_w.shape[0]):
        mod = _ada_modulation(c, ada_w[l], ada_b[l])
        sh1, sc1, g1, sh2, sc2, g2 = [mod[:, i * d:(i + 1) * d] for i in range(6)]

        h = _norm_mod(xf, norm_mix_w[l], sc1, sh1, seq)
        w_in_l = w_in[l].astype(BF16)
        proj = _in_projection(h, w_in_l, cos, sin, main_tiles, main_rope, head_major=False, bn=bn)
        heads_qkv = _in_projection(h, w_in_l, cos, sin, head_tiles, head_rope, head_major=True, bn=bn)
        y_a = _stick_breaking(proj, bsz, seq, sb_heads, 0, d_sb, 2 * d_sb)
        y_b = _dilated_attention(proj, q0_col, k0_col, v0_col, heads_qkv, bsz, seq, dil_heads)
        merged = _gated_merge(y_a, w_o_sb[l].astype(BF16), y_b, w_o_dil[l].astype(BF16),
                              proj, g_sb_col, g_dil_col)
        xf = _residual_matmul(merged, w_out[l].astype(BF16), xf, g1, seq)

        h = _norm_mod(xf, norm_mlp_w[l], sc2, sh2, seq)
        u = _relu2_matmul(h, w_ff1[l].astype(BF16))
        xf = _residual_matmul_ksplit(u, w_ff2[l].astype(BF16), xf, g2, seq)
    return _final_norm(xf, norm_final_w).reshape(bsz, seq, d)
```

```python
import functools

import jax
import jax.numpy as jnp
import numpy as np
from jax import lax
from jax.experimental import pallas as pl
from jax.experimental.pallas import tpu as pltpu

F32 = jnp.float32
BF16 = jnp.bfloat16

HEAD_DIM = 128
LANES = 128
DIL_GROUPS = ((128, 1), (512, 4), (2048, 16))
N_DIL = len(DIL_GROUPS)
ROPE_THETA = 10000.0
EPS = 1e-6
LOG2_E = 1.4426950408889634
SATURATED = 160.0
VMEM_LIMIT = 56 * 1024 * 1024


def _params(*sem):
    return pltpu.CompilerParams(dimension_semantics=sem, vmem_limit_bytes=VMEM_LIMIT)


def _ada_kernel(c_ref, w_ref, b_ref, o_ref):
    c = c_ref[...]
    s = c * (1.0 / (1.0 + jnp.exp(-c)))
    o_ref[...] = jnp.dot(s.astype(BF16), w_ref[...].astype(BF16),
                         preferred_element_type=F32) + b_ref[...]


def _ada_modulation(c, w, b, *, bn=512):
    bsz, d = c.shape
    n = w.shape[1]
    rows = 8
    c_pad = jnp.zeros((rows, d), F32).at[:bsz].set(c)
    out = pl.pallas_call(
        _ada_kernel,
        grid=(n // bn,),
        in_specs=[pl.BlockSpec((rows, d), lambda j: (0, 0)),
                  pl.BlockSpec((d, bn), lambda j: (0, j)),
                  pl.BlockSpec((1, bn), lambda j: (0, j))],
        out_specs=pl.BlockSpec((rows, bn), lambda j: (0, j)),
        out_shape=jax.ShapeDtypeStruct((rows, n), F32),
        compiler_params=_params("arbitrary"),
        name="ada_modulation",
    )(c_pad, w, b.reshape(1, n))
    return out[:bsz]


def _norm_mod_kernel(x_ref, w_ref, sc_ref, sh_ref, o_ref):
    x = x_ref[...]
    inv = lax.rsqrt(jnp.mean(x * x, axis=-1, keepdims=True) + EPS)
    mul = w_ref[...] * (1.0 + sc_ref[...])
    o_ref[...] = (x * inv * mul + sh_ref[...]).astype(o_ref.dtype)


def _norm_mod(x2d, w, sc, sh, seq, *, tm=512):
    n, d = x2d.shape
    bsz = sc.shape[0]
    per_seq = seq // tm
    return pl.pallas_call(
        _norm_mod_kernel,
        grid=(n // tm,),
        in_specs=[pl.BlockSpec((tm, d), lambda i: (i, 0)),
                  pl.BlockSpec((1, d), lambda i: (0, 0)),
                  pl.BlockSpec((None, 1, d), lambda i: (i // per_seq, 0, 0)),
                  pl.BlockSpec((None, 1, d), lambda i: (i // per_seq, 0, 0))],
        out_specs=pl.BlockSpec((tm, d), lambda i: (i, 0)),
        out_shape=jax.ShapeDtypeStruct((n, d), BF16),
        compiler_params=_params("arbitrary"),
        name="norm_modulate",
    )(x2d, w.reshape(1, d), sc.reshape(bsz, 1, d), sh.reshape(bsz, 1, d))


def _final_norm_kernel(x_ref, w_ref, o_ref):
    x = x_ref[...]
    inv = lax.rsqrt(jnp.mean(x * x, axis=-1, keepdims=True) + EPS)
    o_ref[...] = x * inv * w_ref[...]


def _final_norm(x2d, w, *, tm=512):
    n, d = x2d.shape
    return pl.pallas_call(
        _final_norm_kernel,
        grid=(n // tm,),
        in_specs=[pl.BlockSpec((tm, d), lambda i: (i, 0)),
                  pl.BlockSpec((1, d), lambda i: (0, 0))],
        out_specs=pl.BlockSpec((tm, d), lambda i: (i, 0)),
        out_shape=jax.ShapeDtypeStruct((n, d), F32),
        compiler_params=_params("arbitrary"),
        name="final_norm",
    )(x2d, w.reshape(1, d))


def _rope_table_kernel(pos_ref, invf_ref, sign_ref, cos_ref, sin_ref):
    ang = pos_ref[...].astype(F32) * invf_ref[...]
    cos_ref[...] = jnp.cos(ang)
    sin_ref[...] = jnp.sin(ang) * sign_ref[...]


def _rope_tables(positions, *, tm=2048):
    n = positions.size
    half = HEAD_DIM // 2
    inv_freq = ROPE_THETA ** (-jnp.arange(half, dtype=F32) / half)
    invf = jnp.concatenate([inv_freq, inv_freq]).reshape(1, HEAD_DIM)
    sign = jnp.concatenate([-jnp.ones((half,), F32), jnp.ones((half,), F32)]).reshape(1, HEAD_DIM)
    tm = min(tm, n)
    return pl.pallas_call(
        _rope_table_kernel,
        grid=(n // tm,),
        in_specs=[pl.BlockSpec((tm, 1), lambda i: (i, 0)),
                  pl.BlockSpec((1, HEAD_DIM), lambda i: (0, 0)),
                  pl.BlockSpec((1, HEAD_DIM), lambda i: (0, 0))],
        out_specs=[pl.BlockSpec((tm, HEAD_DIM), lambda i: (i, 0)),
                   pl.BlockSpec((tm, HEAD_DIM), lambda i: (i, 0))],
        out_shape=[jax.ShapeDtypeStruct((n, HEAD_DIM), F32),
                   jax.ShapeDtypeStruct((n, HEAD_DIM), F32)],
        compiler_params=_params("arbitrary"),
        name="rope_tables",
    )(positions.reshape(n, 1), invf, sign)


def _inproj_kernel(tile_ref, rope_ref, a_ref, w_ref, cos_ref, sin_ref, *rest, head_major, n_cast):
    del tile_ref
    o_ref = rest[n_cast]
    for src_ref, dst_ref in zip(rest[:n_cast], rest[n_cast + 1:]):
        dst_ref[...] = src_ref[...].astype(dst_ref.dtype)
    rope = rope_ref[pl.program_id(1)] == 1
    acc = jnp.dot(a_ref[...], w_ref[...], preferred_element_type=F32)
    cos = cos_ref[...]
    sin = sin_ref[...]
    for h in range(acc.shape[1] // HEAD_DIM):
        xh = acc[:, h * HEAD_DIM:(h + 1) * HEAD_DIM]
        val = jnp.where(rope, xh * cos + pltpu.roll(xh, HEAD_DIM // 2, 1) * sin, xh)
        if head_major:
            o_ref[h] = val.astype(o_ref.dtype)
        else:
            o_ref[:, h * HEAD_DIM:(h + 1) * HEAD_DIM] = val.astype(o_ref.dtype)


def _in_projection(h, w, cos, sin, col_tiles, rope_flags, *, head_major, to_bf16=(), cast_blocks=1,
                   bm=1024, bn=1024):
    m, k = h.shape
    nt = len(col_tiles)
    assert cast_blocks <= (m // bm) * nt
    if head_major:
        out_spec = pl.BlockSpec((bn // HEAD_DIM, bm, HEAD_DIM), lambda i, j, t, r: (j, i, 0))
        out_shape = jax.ShapeDtypeStruct((nt * (bn // HEAD_DIM), m, HEAD_DIM), F32)
    else:
        out_spec = pl.BlockSpec((bm, bn), lambda i, j, t, r: (i, j))
        out_shape = jax.ShapeDtypeStruct((m, nt * bn), BF16)

    def cast_spec(mat):
        rows, cols = mat.shape
        return pl.BlockSpec((rows // cast_blocks, cols),
                            lambda i, j, t, r: (jnp.minimum(i * nt + j, cast_blocks - 1), 0))

    outs = pl.pallas_call(
        functools.partial(_inproj_kernel, head_major=head_major, n_cast=len(to_bf16)),
        grid_spec=pltpu.PrefetchScalarGridSpec(
            num_scalar_prefetch=2,
            grid=(m // bm, nt),
            in_specs=[pl.BlockSpec((bm, k), lambda i, j, t, r: (i, 0)),
                      pl.BlockSpec((k, bn), lambda i, j, t, r: (0, t[j])),
                      pl.BlockSpec((bm, HEAD_DIM), lambda i, j, t, r: (i, 0)),
                      pl.BlockSpec((bm, HEAD_DIM), lambda i, j, t, r: (i, 0))]
                     + [cast_spec(mat) for mat in to_bf16],
            out_specs=[out_spec] + [cast_spec(mat) for mat in to_bf16]),
        out_shape=[out_shape] + [jax.ShapeDtypeStruct(mat.shape, BF16) for mat in to_bf16],
        compiler_params=_params("arbitrary", "arbitrary"),
        name="in_projection_heads" if head_major else "in_projection",
    )(jnp.asarray(col_tiles, jnp.int32), jnp.asarray(rope_flags, jnp.int32), h, w, cos, sin, *to_bf16)
    return outs[0], outs[1:]


def _merge_kernel(ya_ref, wa_ref, yb_ref, wb_ref, ga_ref, gb_ref, o_ref):
    a = jnp.dot(ya_ref[...], wa_ref[...], preferred_element_type=F32)
    b = jnp.dot(yb_ref[...], wb_ref[...], preferred_element_type=F32)
    ga = ga_ref[...].astype(F32)
    gb = gb_ref[...].astype(F32)
    o_ref[...] = (a / (1.0 + jnp.exp(-ga)) + b / (1.0 + jnp.exp(-gb))).astype(o_ref.dtype)


def _gated_merge(ya, wa, yb, wb, proj, ga_col, gb_col, *, bm=1024, bn=1024):
    m = ya.shape[0]
    n = wa.shape[1]
    ga_blk, gb_blk = ga_col // bn, gb_col // bn
    return pl.pallas_call(
        _merge_kernel,
        grid=(m // bm, n // bn),
        in_specs=[pl.BlockSpec((bm, ya.shape[1]), lambda i, j: (i, 0)),
                  pl.BlockSpec((wa.shape[0], bn), lambda i, j: (0, j)),
                  pl.BlockSpec((bm, yb.shape[1]), lambda i, j: (i, 0)),
                  pl.BlockSpec((wb.shape[0], bn), lambda i, j: (0, j)),
                  pl.BlockSpec((bm, bn), lambda i, j: (i, ga_blk + j)),
                  pl.BlockSpec((bm, bn), lambda i, j: (i, gb_blk + j))],
        out_specs=pl.BlockSpec((bm, bn), lambda i, j: (i, j)),
        out_shape=jax.ShapeDtypeStruct((m, n), BF16),
        compiler_params=_params("arbitrary", "arbitrary"),
        name="gated_merge",
    )(ya, wa, yb, wb, proj, proj)


def _residual_mm_kernel(a_ref, w_ref, x_ref, g_ref, o_ref):
    acc = jnp.dot(a_ref[...], w_ref[...], preferred_element_type=F32)
    o_ref[...] = x_ref[...] + g_ref[...] * acc


def _residual_matmul(a, w, x2d, gate, seq, *, bm=1024, bn=1024):
    m, k = a.shape
    n = w.shape[1]
    bsz = gate.shape[0]
    per_seq = seq // bm
    return pl.pallas_call(
        _residual_mm_kernel,
        grid=(m // bm, n // bn),
        in_specs=[pl.BlockSpec((bm, k), lambda i, j: (i, 0)),
                  pl.BlockSpec((k, bn), lambda i, j: (0, j)),
                  pl.BlockSpec((bm, bn), lambda i, j: (i, j)),
                  pl.BlockSpec((None, 1, bn), lambda i, j: (i // per_seq, 0, j))],
        out_specs=pl.BlockSpec((bm, bn), lambda i, j: (i, j)),
        out_shape=jax.ShapeDtypeStruct((m, n), F32),
        compiler_params=_params("arbitrary", "arbitrary"),
        name="out_projection",
    )(a, w, x2d, gate.reshape(bsz, 1, n))


def _relu2_mm_kernel(a_ref, w_ref, o_ref):
    acc = jnp.dot(a_ref[...], w_ref[...], preferred_element_type=F32)
    r = jnp.maximum(acc, 0.0)
    o_ref[...] = (r * r).astype(o_ref.dtype)


def _relu2_matmul(a, w, *, bm=1024, bn=1024):
    m, k = a.shape
    n = w.shape[1]
    return pl.pallas_call(
        _relu2_mm_kernel,
        grid=(m // bm, n // bn),
        in_specs=[pl.BlockSpec((bm, k), lambda i, j: (i, 0)),
                  pl.BlockSpec((k, bn), lambda i, j: (0, j))],
        out_specs=pl.BlockSpec((bm, bn), lambda i, j: (i, j)),
        out_shape=jax.ShapeDtypeStruct((m, n), BF16),
        compiler_params=_params("arbitrary", "arbitrary"),
        name="mlp_up",
    )(a, w)


def _residual_mm_ksplit_kernel(a_ref, w_ref, x_ref, g_ref, o_ref):
    kk = pl.program_id(2)

    @pl.when(kk == 0)
    def _():
        o_ref[...] = jnp.zeros_like(o_ref)

    o_ref[...] += jnp.dot(a_ref[...], w_ref[...], preferred_element_type=F32)

    @pl.when(kk == pl.num_programs(2) - 1)
    def _():
        o_ref[...] = x_ref[...] + g_ref[...] * o_ref[...]


def _residual_matmul_ksplit(a, w, x2d, gate, seq, *, bm=1024, bn=1024, bk=4096):
    m, k = a.shape
    n = w.shape[1]
    bsz = gate.shape[0]
    per_seq = seq // bm
    return pl.pallas_call(
        _residual_mm_ksplit_kernel,
        grid=(m // bm, n // bn, k // bk),
        in_specs=[pl.BlockSpec((bm, bk), lambda i, j, kk: (i, kk)),
                  pl.BlockSpec((bk, bn), lambda i, j, kk: (kk, j)),
                  pl.BlockSpec((bm, bn), lambda i, j, kk: (i, j)),
                  pl.BlockSpec((None, 1, bn), lambda i, j, kk: (i // per_seq, 0, j))],
        out_specs=pl.BlockSpec((bm, bn), lambda i, j, kk: (i, j)),
        out_shape=jax.ShapeDtypeStruct((m, n), F32),
        compiler_params=_params("arbitrary", "arbitrary", "arbitrary"),
        name="mlp_down",
    )(a, w, x2d, gate.reshape(bsz, 1, n))


def _suffix_sum_matrix():
    r = np.arange(2 * LANES)[:, None] % LANES
    c = np.arange(2 * LANES)[None, :]
    return jnp.asarray(np.where(c < LANES, r >= c, True), dtype=BF16)


def _sb_kernel(q_ref, k_ref, v_ref, ut_ref, o_ref, acc_ref, carry_ref, *, tq, tk, scale):
    seq = q_ref.shape[0]
    n_heads = q_ref.shape[1] // HEAD_DIM
    n_diag = tq // tk
    z_scale = scale * LOG2_E

    def visit(hd, q0, r0, nr, k0, triangle):
        hcols = slice(hd * HEAD_DIM, (hd + 1) * HEAD_DIM)
        rows = slice(r0, r0 + nr)
        q = q_ref[pl.ds(pl.multiple_of(q0 + r0, tk), nr), hcols]
        k = k_ref[pl.ds(k0, tk), hcols]
        v = v_ref[pl.ds(k0, tk), hcols]
        z2 = lax.dot_general(q, k, (((1,), (1,)), ((), ())), preferred_element_type=F32) * z_scale
        nlr = jnp.maximum(z2, 0.0) + jnp.log2(1.0 + jnp.exp2(-jnp.abs(z2)))
        if triangle:
            row = lax.broadcasted_iota(jnp.int32, (nr, tk), 0)
            col = lax.broadcasted_iota(jnp.int32, (nr, tk), 1)
            causal = col < row
            nlr = jnp.where(causal, nlr, 0.0)
        carry = carry_ref[hd, rows, :]
        parts = [None] * (tk // LANES)
        for sb in reversed(range(tk // LANES)):
            cols = slice(sb * LANES, (sb + 1) * LANES)
            x = nlr[:, cols]
            hi = x.astype(BF16)
            lo = (x - hi.astype(F32)).astype(BF16)
            cs = jnp.dot(jnp.concatenate([hi, lo], axis=1), ut_ref[...], preferred_element_type=F32)
            a = jnp.exp2(z2[:, cols] - cs[:, :LANES] - carry)
            if triangle:
                a = jnp.where(causal[:, cols], a, 0.0)
            parts[sb] = a.astype(BF16)
            carry = carry + cs[:, LANES:]
        carry_ref[hd, rows, :] = carry
        acc_ref[hd, rows, :] += jnp.dot(jnp.concatenate(parts, axis=1), v, preferred_element_type=F32)

    def q_block(iq, _):
        q0 = pl.multiple_of(iq * tq, tq)
        acc_ref[...] = jnp.zeros_like(acc_ref)
        carry_ref[...] = jnp.zeros_like(carry_ref)
        for d in reversed(range(n_diag)):
            for hd in range(n_heads):
                visit(hd, q0, d * tk, tk, q0 + d * tk, True)
                if d < n_diag - 1:
                    visit(hd, q0, (d + 1) * tk, tq - (d + 1) * tk, q0 + d * tk, False)

        def more(state):
            t, lowest = state
            return jnp.logical_and(t < iq * n_diag, lowest < SATURATED)

        def left(state):
            t, _ = state
            k0 = pl.multiple_of(q0 - (t + 1) * tk, tk)
            for hd in range(n_heads):
                visit(hd, q0, 0, tq, k0, False)
            return t + 1, jnp.min(carry_ref[...])

        lax.while_loop(more, left, (jnp.int32(0), jnp.min(carry_ref[...])))
        for hd in range(n_heads):
            o_ref[pl.ds(q0, tq), hd * HEAD_DIM:(hd + 1) * HEAD_DIM] = acc_ref[hd].astype(o_ref.dtype)
        return 0

    lax.fori_loop(0, seq // tq, q_block, 0)


def _stick_breaking(proj, bsz, seq, n_heads, q_col, k_col, v_col, *, tq=512, tk=256, heads_per_step=4):
    hps = min(heads_per_step, n_heads)
    wb = hps * HEAD_DIM
    qb, kb, vb = q_col // wb, k_col // wb, v_col // wb
    kern = functools.partial(_sb_kernel, tq=tq, tk=tk, scale=HEAD_DIM ** -0.5)
    return pl.pallas_call(
        kern,
        grid=(bsz, n_heads // hps),
        in_specs=[pl.BlockSpec((seq, wb), lambda b, h: (b, qb + h)),
                  pl.BlockSpec((seq, wb), lambda b, h: (b, kb + h)),
                  pl.BlockSpec((seq, wb), lambda b, h: (b, vb + h)),
                  pl.BlockSpec((2 * LANES, 2 * LANES), lambda b, h: (0, 0))],
        out_specs=pl.BlockSpec((seq, wb), lambda b, h: (b, h)),
        out_shape=jax.ShapeDtypeStruct((bsz * seq, n_heads * HEAD_DIM), BF16),
        scratch_shapes=[pltpu.VMEM((hps, tq, HEAD_DIM), F32), pltpu.VMEM((hps, tq, LANES), F32)],
        compiler_params=_params("arbitrary", "arbitrary"),
        name="stick_breaking",
    )(proj, proj, proj, _suffix_sum_matrix())


WINDOW = 128
DIL_TILE = 1024
MIN_PROBLEM_ROWS = 256


def _window_softmax(qs, ks, vs, valid, scale):
    scores = []
    for q, k in zip(qs, ks):
        s = lax.dot_general(q, k, (((1,), (1,)), ((), ())), preferred_element_type=F32) * scale
        scores.append(jnp.where(valid, s, -jnp.inf))
    stats = []
    for s in scores:
        m = jnp.max(s, axis=1, keepdims=True)
        p = jnp.exp(s - m)
        stats.append((p.astype(BF16), m, jnp.sum(p, axis=1, keepdims=True)))
    return [(jnp.dot(p, v, preferred_element_type=F32) / l, m + jnp.log(l))
            for (p, m, l), v in zip(stats, vs)]


def _dil_strided_kernel(q_ref, k_ref, v_ref, o_ref, lse_ref, qd_ref, kd_ref, vd_ref, *, dil, group, scale):
    n_heads, tile, _ = q_ref.shape
    chunk = tile // dil
    n_hist = -(-WINDOW // chunk)
    n_slots = n_hist + 1
    takes = [min(chunk, WINDOW - (back - 1) * chunk) for back in range(1, n_hist + 1)]
    assert group == 1 or all(t == chunk for t in takes)
    seg = group * chunk
    n_keys = group * (WINDOW + chunk)
    kt = pl.program_id(1)
    slot = lax.rem(kt, n_slots)

    @pl.when(kt == 0)
    def _():
        kd_ref[...] = jnp.zeros_like(kd_ref)
        vd_ref[...] = jnp.zeros_like(vd_ref)

    def deinterleave(c, _):
        rows = pl.ds(pl.multiple_of(c * chunk, chunk), chunk)
        for h in range(n_heads):
            qd_ref[h, rows, :] = q_ref[h, pl.ds(c, chunk, stride=dil), :].astype(BF16)
            kd_ref[slot, h, rows, :] = k_ref[h, pl.ds(c, chunk, stride=dil), :].astype(BF16)
            vd_ref[slot, h, rows, :] = v_ref[h, pl.ds(c, chunk, stride=dil), :].astype(BF16)
        return 0

    lax.fori_loop(0, dil, deinterleave, 0)

    row = lax.broadcasted_iota(jnp.int32, (seg, n_keys), 0)
    col = lax.broadcasted_iota(jnp.int32, (seg, n_keys), 1)
    if group == 1:
        i, u, same_class = row, col, None
    else:
        lc, ls = chunk.bit_length() - 1, seg.bit_length() - 1
        i = row & (chunk - 1)
        within = col & (seg - 1)
        u = ((col >> ls) << lc) + (within & (chunk - 1))
        same_class = (within >> lc) == (row >> lc)
    valid = jnp.logical_and(u >= i, u <= i + WINDOW)
    valid = jnp.logical_and(valid, u >= WINDOW - chunk * kt)
    if same_class is not None:
        valid = jnp.logical_and(valid, same_class)
    lane = lax.broadcasted_iota(jnp.int32, (seg, LANES), 1)

    def attend(gi, _):
        base = pl.multiple_of(gi * seg, seg)
        qs, ks, vs = [], [], []
        for h in range(n_heads):
            kparts, vparts = [], []
            for back in range(n_hist, 0, -1):
                take = takes[back - 1]
                old = lax.rem(kt + (n_slots - back), n_slots)
                rows = pl.ds(pl.multiple_of(base + (chunk - take), 16), take if group == 1 else seg)
                kparts.append(kd_ref[old, h, rows, :])
                vparts.append(vd_ref[old, h, rows, :])
            kparts.append(kd_ref[slot, h, pl.ds(base, seg), :])
            vparts.append(vd_ref[slot, h, pl.ds(base, seg), :])
            qs.append(qd_ref[h, pl.ds(base, seg), :])
            ks.append(jnp.concatenate(kparts, axis=0))
            vs.append(jnp.concatenate(vparts, axis=0))
        lse_blk = jnp.zeros((seg, LANES), F32)
        for h, (o, lse) in enumerate(_window_softmax(qs, ks, vs, valid, scale)):
            for cc in range(group):
                o_ref[h, pl.ds(gi * group + cc, chunk, stride=dil), :] = o[cc * chunk:(cc + 1) * chunk]
            lse_blk = jnp.where(lane == h, lse, lse_blk)
        for cc in range(group):
            lse_ref[pl.ds(gi * group + cc, chunk, stride=dil), :] = lse_blk[cc * chunk:(cc + 1) * chunk]
        return 0

    lax.fori_loop(0, dil // group, attend, 0)


def _dil_strided(heads_qkv, g, bsz, seq, n_heads):
    n = heads_qkv.shape[1]
    dil = DIL_GROUPS[g][1]
    tile = min(DIL_TILE, seq)
    chunk = tile // dil
    n_slots = -(-WINDOW // chunk) + 1
    group = max(1, min(dil, MIN_PROBLEM_ROWS // chunk))
    per_seq = seq // tile

    def operand(which):
        return pl.BlockSpec((n_heads, tile, HEAD_DIM), lambda b, t: (2 * which + g - 1, b * per_seq + t, 0))

    return pl.pallas_call(
        functools.partial(_dil_strided_kernel, dil=dil, group=group, scale=HEAD_DIM ** -0.5),
        grid=(bsz, per_seq),
        in_specs=[operand(0), operand(1), operand(2)],
        out_specs=[pl.BlockSpec((n_heads, tile, HEAD_DIM), lambda b, t: (0, b * per_seq + t, 0)),
                   pl.BlockSpec((tile, LANES), lambda b, t: (b * per_seq + t, 0))],
        out_shape=[jax.ShapeDtypeStruct((n_heads, n, HEAD_DIM), F32),
                   jax.ShapeDtypeStruct((n, LANES), F32)],
        scratch_shapes=[pltpu.VMEM((n_heads, tile, HEAD_DIM), BF16),
                        pltpu.VMEM((n_slots, n_heads, tile, HEAD_DIM), BF16),
                        pltpu.VMEM((n_slots, n_heads, tile, HEAD_DIM), BF16)],
        compiler_params=_params("arbitrary", "arbitrary"),
        name=f"dilated_stride{dil}",
    )(heads_qkv, heads_qkv, heads_qkv)


def _dil_merge_kernel(q_ref, kp_ref, kc_ref, vp_ref, vc_ref, o2_ref, l2_ref, o3_ref, l3_ref, y_ref,
                      *, scale):
    tile = q_ref.shape[0]
    n_heads = q_ref.shape[1] // HEAD_DIM
    blk = WINDOW
    first = pl.program_id(1) == 0
    row = lax.broadcasted_iota(jnp.int32, (blk, 2 * blk), 0)
    col = lax.broadcasted_iota(jnp.int32, (blk, 2 * blk), 1)
    band = jnp.logical_and(col >= row, col <= row + blk)
    band_first = jnp.logical_and(band, jnp.logical_or(col >= blk, jnp.logical_not(first)))
    for sub in range(tile // blk):
        rows = slice(sub * blk, (sub + 1) * blk)
        heads = [slice(h * HEAD_DIM, (h + 1) * HEAD_DIM) for h in range(n_heads)]
        if sub == 0:
            ks = [jnp.concatenate([kp_ref[:, cols], kc_ref[rows, cols]], axis=0) for cols in heads]
            vs = [jnp.concatenate([vp_ref[:, cols], vc_ref[rows, cols]], axis=0) for cols in heads]
            valid = band_first
        else:
            ks = [kc_ref[(sub - 1) * blk:(sub + 1) * blk, cols] for cols in heads]
            vs = [vc_ref[(sub - 1) * blk:(sub + 1) * blk, cols] for cols in heads]
            valid = band
        for h in range(n_heads):
            (o1, lse1), = _window_softmax([q_ref[rows, heads[h]]], ks[h:h + 1], vs[h:h + 1], valid, scale)
            lse2 = l2_ref[rows, h:h + 1]
            lse3 = l3_ref[rows, h:h + 1]
            mx = jnp.maximum(jnp.maximum(lse1, lse2), lse3)
            w1, w2, w3 = jnp.exp(lse1 - mx), jnp.exp(lse2 - mx), jnp.exp(lse3 - mx)
            y = (w1 * o1 + w2 * o2_ref[h, rows, :] + w3 * o3_ref[h, rows, :]) / (w1 + w2 + w3)
            y_ref[rows, heads[h]] = y.astype(y_ref.dtype)


def _dil_merge(proj, q_col, k_col, v_col, o2, l2, o3, l3, bsz, seq, n_heads, *, tile=512):
    n = proj.shape[0]
    gw = n_heads * HEAD_DIM
    tile = min(tile, seq)
    per_seq = seq // tile
    sub_per_tile = tile // WINDOW
    sub_per_seq = seq // WINDOW
    qo, ko, vo = q_col // gw, k_col // gw, v_col // gw

    def cur(off):
        return pl.BlockSpec((tile, gw), lambda b, t: (b * per_seq + t, off))

    def prev(off):
        return pl.BlockSpec(
            (WINDOW, gw), lambda b, t: (b * sub_per_seq + jnp.maximum(t * sub_per_tile - 1, 0), off))

    heads = pl.BlockSpec((n_heads, tile, HEAD_DIM), lambda b, t: (0, b * per_seq + t, 0))
    lses = pl.BlockSpec((tile, LANES), lambda b, t: (b * per_seq + t, 0))
    return pl.pallas_call(
        functools.partial(_dil_merge_kernel, scale=HEAD_DIM ** -0.5),
        grid=(bsz, per_seq),
        in_specs=[cur(qo), prev(ko), cur(ko), prev(vo), cur(vo), heads, lses, heads, lses],
        out_specs=pl.BlockSpec((tile, gw), lambda b, t: (b * per_seq + t, 0)),
        out_shape=jax.ShapeDtypeStruct((n, gw), BF16),
        compiler_params=_params("arbitrary", "arbitrary"),
        name="dilated_merge",
    )(proj, proj, proj, proj, proj, o2, l2, o3, l3)


def _dilated_attention(proj, q_col, k_col, v_col, heads_qkv, bsz, seq, n_heads):
    o3, l3 = _dil_strided(heads_qkv, 2, bsz, seq, n_heads)
    o2, l2 = _dil_strided(heads_qkv, 1, bsz, seq, n_heads)
    return _dil_merge(proj, q_col, k_col, v_col, o2, l2, o3, l3, bsz, seq, n_heads)


def kernel(x, c, positions, ada_w, ada_b, norm_mix_w, w_in, w_o_sb, w_o_dil, w_out,
           norm_mlp_w, w_ff1, w_ff2, norm_final_w):
    bsz, seq, d = x.shape
    n = bsz * seq
    d_sb = w_o_sb.shape[1]
    d_dil = w_o_dil.shape[1]
    sb_heads = d_sb // HEAD_DIM
    dil_heads = d_dil // HEAD_DIM
    bn = d_dil
    sb_t = 3 * d_sb // bn
    dil_t = lambda which, g: sb_t + which * N_DIL + g
    gate_t = sb_t + 3 * N_DIL
    main_tiles = (list(range(sb_t)) + [dil_t(0, 0), dil_t(1, 0), dil_t(2, 0)]
                  + list(range(gate_t, gate_t + 2 * d // bn)))
    main_rope = [0] * sb_t + [1, 1, 0] + [0] * (2 * d // bn)
    q0_col, k0_col, v0_col = sb_t * bn, (sb_t + 1) * bn, (sb_t + 2) * bn
    g_sb_col = (sb_t + 3) * bn
    g_dil_col = g_sb_col + d
    head_tiles = [dil_t(which, g) for which in range(3) for g in (1, 2)]
    head_rope = [1, 1, 1, 1, 0, 0]

    cos, sin = _rope_tables(positions)
    xf = x.reshape(n, d)
    for l in range(ada_w.shape[0]):
        mod = _ada_modulation(c, ada_w[l], ada_b[l])
        sh1, sc1, g1, sh2, sc2, g2 = [mod[:, i * d:(i + 1) * d] for i in range(6)]

        h = _norm_mod(xf, norm_mix_w[l], sc1, sh1, seq)
        w_in_l = w_in[l].astype(BF16)
        proj, (w_ff1_l, w_ff2_l) = _in_projection(
            h, w_in_l, cos, sin, main_tiles, main_rope, head_major=False, bn=bn,
            to_bf16=(w_ff1[l], w_ff2[l]), cast_blocks=256)
        heads_qkv, _ = _in_projection(h, w_in_l, cos, sin, head_tiles, head_rope, head_major=True, bn=bn)
        y_a = _stick_breaking(proj, bsz, seq, sb_heads, 0, d_sb, 2 * d_sb)
        y_b = _dilated_attention(proj, q0_col, k0_col, v0_col, heads_qkv, bsz, seq, dil_heads)
        merged = _gated_merge(y_a, w_o_sb[l].astype(BF16), y_b, w_o_dil[l].astype(BF16),
                              proj, g_sb_col, g_dil_col)
        xf = _residual_matmul(merged, w_out[l].astype(BF16), xf, g1, seq)

        h = _norm_mod(xf, norm_mlp_w[l], sc2, sh2, seq)
        u = _relu2_matmul(h, w_ff1_l)
        xf = _residual_matmul_ksplit(u, w_ff2_l, xf, g2, seq)
    return _final_norm(xf, norm_final_w).reshape(bsz, seq, d)
```

```python
import functools

import jax
import jax.numpy as jnp
import numpy as np
from jax import lax
from jax.experimental import pallas as pl
from jax.experimental.pallas import tpu as pltpu

F32 = jnp.float32
BF16 = jnp.bfloat16

HEAD_DIM = 128
LANES = 128
BF16_ROWS = 16
DIL_GROUPS = ((128, 1), (512, 4), (2048, 16))
N_DIL = len(DIL_GROUPS)
ROPE_THETA = 10000.0
EPS = 1e-6
LOG2_E = 1.4426950408889634
SATURATED = 160.0
VMEM_LIMIT = 56 * 1024 * 1024


def _params(*sem):
    return pltpu.CompilerParams(dimension_semantics=sem, vmem_limit_bytes=VMEM_LIMIT)


def _ada_kernel(c_ref, w_ref, b_ref, o_ref):
    c = c_ref[...]
    s = c * (1.0 / (1.0 + jnp.exp(-c)))
    o_ref[...] = jnp.dot(s.astype(BF16), w_ref[...].astype(BF16),
                         preferred_element_type=F32) + b_ref[...]


def _ada_modulation(c, w, b, *, bn=512):
    bsz, d = c.shape
    n = w.shape[1]
    rows = 8
    c_pad = jnp.zeros((rows, d), F32).at[:bsz].set(c)
    out = pl.pallas_call(
        _ada_kernel,
        grid=(n // bn,),
        in_specs=[pl.BlockSpec((rows, d), lambda j: (0, 0)),
                  pl.BlockSpec((d, bn), lambda j: (0, j)),
                  pl.BlockSpec((1, bn), lambda j: (0, j))],
        out_specs=pl.BlockSpec((rows, bn), lambda j: (0, j)),
        out_shape=jax.ShapeDtypeStruct((rows, n), F32),
        compiler_params=_params("arbitrary"),
        name="ada_modulation",
    )(c_pad, w, b.reshape(1, n))
    return out[:bsz]


def _norm_mod_kernel(x_ref, w_ref, sc_ref, sh_ref, o_ref):
    x = x_ref[...]
    inv = lax.rsqrt(jnp.mean(x * x, axis=-1, keepdims=True) + EPS)
    mul = w_ref[...] * (1.0 + sc_ref[...])
    o_ref[...] = (x * inv * mul + sh_ref[...]).astype(o_ref.dtype)


def _norm_mod(x2d, w, sc, sh, seq, *, tm=512):
    n, d = x2d.shape
    bsz = sc.shape[0]
    per_seq = seq // tm
    return pl.pallas_call(
        _norm_mod_kernel,
        grid=(n // tm,),
        in_specs=[pl.BlockSpec((tm, d), lambda i: (i, 0)),
                  pl.BlockSpec((1, d), lambda i: (0, 0)),
                  pl.BlockSpec((None, 1, d), lambda i: (i // per_seq, 0, 0)),
                  pl.BlockSpec((None, 1, d), lambda i: (i // per_seq, 0, 0))],
        out_specs=pl.BlockSpec((tm, d), lambda i: (i, 0)),
        out_shape=jax.ShapeDtypeStruct((n, d), BF16),
        compiler_params=_params("arbitrary"),
        name="norm_modulate",
    )(x2d, w.reshape(1, d), sc.reshape(bsz, 1, d), sh.reshape(bsz, 1, d))


def _final_norm_kernel(x_ref, w_ref, o_ref):
    x = x_ref[...]
    inv = lax.rsqrt(jnp.mean(x * x, axis=-1, keepdims=True) + EPS)
    o_ref[...] = x * inv * w_ref[...]


def _final_norm(x2d, w, *, tm=512):
    n, d = x2d.shape
    return pl.pallas_call(
        _final_norm_kernel,
        grid=(n // tm,),
        in_specs=[pl.BlockSpec((tm, d), lambda i: (i, 0)),
                  pl.BlockSpec((1, d), lambda i: (0, 0))],
        out_specs=pl.BlockSpec((tm, d), lambda i: (i, 0)),
        out_shape=jax.ShapeDtypeStruct((n, d), F32),
        compiler_params=_params("arbitrary"),
        name="final_norm",
    )(x2d, w.reshape(1, d))


def _rope_table_kernel(pos_ref, invf_ref, sign_ref, cos_ref, sin_ref):
    ang = pos_ref[...].astype(F32) * invf_ref[...]
    cos_ref[...] = jnp.cos(ang)
    sin_ref[...] = jnp.sin(ang) * sign_ref[...]


def _rope_tables(positions, *, tm=2048):
    n = positions.size
    half = HEAD_DIM // 2
    inv_freq = ROPE_THETA ** (-jnp.arange(half, dtype=F32) / half)
    invf = jnp.concatenate([inv_freq, inv_freq]).reshape(1, HEAD_DIM)
    sign = jnp.concatenate([-jnp.ones((half,), F32), jnp.ones((half,), F32)]).reshape(1, HEAD_DIM)
    tm = min(tm, n)
    return pl.pallas_call(
        _rope_table_kernel,
        grid=(n // tm,),
        in_specs=[pl.BlockSpec((tm, 1), lambda i: (i, 0)),
                  pl.BlockSpec((1, HEAD_DIM), lambda i: (0, 0)),
                  pl.BlockSpec((1, HEAD_DIM), lambda i: (0, 0))],
        out_specs=[pl.BlockSpec((tm, HEAD_DIM), lambda i: (i, 0)),
                   pl.BlockSpec((tm, HEAD_DIM), lambda i: (i, 0))],
        out_shape=[jax.ShapeDtypeStruct((n, HEAD_DIM), F32),
                   jax.ShapeDtypeStruct((n, HEAD_DIM), F32)],
        compiler_params=_params("arbitrary"),
        name="rope_tables",
    )(positions.reshape(n, 1), invf, sign)


def _inproj_kernel(tile_ref, rope_ref, a_ref, w_ref, cos_ref, sin_ref, *rest, head_major, n_cast):
    del tile_ref
    o_ref = rest[n_cast]
    for src_ref, dst_ref in zip(rest[:n_cast], rest[n_cast + 1:]):
        dst_ref[...] = src_ref[...].astype(dst_ref.dtype)
    rope = rope_ref[pl.program_id(1)] == 1
    acc = jnp.dot(a_ref[...], w_ref[...], preferred_element_type=F32)
    cos = cos_ref[...]
    sin = sin_ref[...]
    for h in range(acc.shape[1] // HEAD_DIM):
        xh = acc[:, h * HEAD_DIM:(h + 1) * HEAD_DIM]
        val = jnp.where(rope, xh * cos + pltpu.roll(xh, HEAD_DIM // 2, 1) * sin, xh)
        if head_major:
            o_ref[h] = val.astype(o_ref.dtype)
        else:
            o_ref[:, h * HEAD_DIM:(h + 1) * HEAD_DIM] = val.astype(o_ref.dtype)


def _in_projection(h, w, cos, sin, col_tiles, rope_flags, *, head_major, to_bf16=(), bm=1024, bn=1024):
    m, k = h.shape
    nt = len(col_tiles)
    n_steps = (m // bm) * nt
    if head_major:
        out_spec = pl.BlockSpec((bn // HEAD_DIM, bm, HEAD_DIM), lambda i, j, t, r: (j, i, 0))
        out_shape = jax.ShapeDtypeStruct((nt * (bn // HEAD_DIM), m, HEAD_DIM), F32)
    else:
        out_spec = pl.BlockSpec((bm, bn), lambda i, j, t, r: (i, j))
        out_shape = jax.ShapeDtypeStruct((m, nt * bn), BF16)

    def cast_spec(mat):
        rows, cols = mat.shape
        blk = BF16_ROWS
        while rows % blk or rows // blk > n_steps:
            blk += BF16_ROWS
        last = rows // blk - 1
        return pl.BlockSpec((blk, cols), lambda i, j, t, r: (jnp.minimum(i * nt + j, last), 0))

    outs = pl.pallas_call(
        functools.partial(_inproj_kernel, head_major=head_major, n_cast=len(to_bf16)),
        grid_spec=pltpu.PrefetchScalarGridSpec(
            num_scalar_prefetch=2,
            grid=(m // bm, nt),
            in_specs=[pl.BlockSpec((bm, k), lambda i, j, t, r: (i, 0)),
                      pl.BlockSpec((k, bn), lambda i, j, t, r: (0, t[j])),
                      pl.BlockSpec((bm, HEAD_DIM), lambda i, j, t, r: (i, 0)),
                      pl.BlockSpec((bm, HEAD_DIM), lambda i, j, t, r: (i, 0))]
                     + [cast_spec(mat) for mat in to_bf16],
            out_specs=[out_spec] + [cast_spec(mat) for mat in to_bf16]),
        out_shape=[out_shape] + [jax.ShapeDtypeStruct(mat.shape, BF16) for mat in to_bf16],
        compiler_params=_params("arbitrary", "arbitrary"),
        name="in_projection_heads" if head_major else "in_projection",
    )(jnp.asarray(col_tiles, jnp.int32), jnp.asarray(rope_flags, jnp.int32), h, w, cos, sin, *to_bf16)
    return outs[0], outs[1:]


def _merge_kernel(ya_ref, wa_ref, yb_ref, wb_ref, ga_ref, gb_ref, o_ref):
    a = jnp.dot(ya_ref[...], wa_ref[...], preferred_element_type=F32)
    b = jnp.dot(yb_ref[...], wb_ref[...], preferred_element_type=F32)
    ga = ga_ref[...].astype(F32)
    gb = gb_ref[...].astype(F32)
    o_ref[...] = (a / (1.0 + jnp.exp(-ga)) + b / (1.0 + jnp.exp(-gb))).astype(o_ref.dtype)


def _gated_merge(ya, wa, yb, wb, proj, ga_col, gb_col, *, bm=1024, bn=1024):
    m = ya.shape[0]
    n = wa.shape[1]
    ga_blk, gb_blk = ga_col // bn, gb_col // bn
    return pl.pallas_call(
        _merge_kernel,
        grid=(m // bm, n // bn),
        in_specs=[pl.BlockSpec((bm, ya.shape[1]), lambda i, j: (i, 0)),
                  pl.BlockSpec((wa.shape[0], bn), lambda i, j: (0, j)),
                  pl.BlockSpec((bm, yb.shape[1]), lambda i, j: (i, 0)),
                  pl.BlockSpec((wb.shape[0], bn), lambda i, j: (0, j)),
                  pl.BlockSpec((bm, bn), lambda i, j: (i, ga_blk + j)),
                  pl.BlockSpec((bm, bn), lambda i, j: (i, gb_blk + j))],
        out_specs=pl.BlockSpec((bm, bn), lambda i, j: (i, j)),
        out_shape=jax.ShapeDtypeStruct((m, n), BF16),
        compiler_params=_params("arbitrary", "arbitrary"),
        name="gated_merge",
    )(ya, wa, yb, wb, proj, proj)


def _residual_mm_kernel(a_ref, w_ref, x_ref, g_ref, o_ref):
    acc = jnp.dot(a_ref[...], w_ref[...], preferred_element_type=F32)
    o_ref[...] = x_ref[...] + g_ref[...] * acc


def _residual_matmul(a, w, x2d, gate, seq, *, bm=1024, bn=1024):
    m, k = a.shape
    n = w.shape[1]
    bsz = gate.shape[0]
    per_seq = seq // bm
    return pl.pallas_call(
        _residual_mm_kernel,
        grid=(m // bm, n // bn),
        in_specs=[pl.BlockSpec((bm, k), lambda i, j: (i, 0)),
                  pl.BlockSpec((k, bn), lambda i, j: (0, j)),
                  pl.BlockSpec((bm, bn), lambda i, j: (i, j)),
                  pl.BlockSpec((None, 1, bn), lambda i, j: (i // per_seq, 0, j))],
        out_specs=pl.BlockSpec((bm, bn), lambda i, j: (i, j)),
        out_shape=jax.ShapeDtypeStruct((m, n), F32),
        compiler_params=_params("arbitrary", "arbitrary"),
        name="out_projection",
    )(a, w, x2d, gate.reshape(bsz, 1, n))


def _relu2_mm_kernel(a_ref, w_ref, o_ref):
    acc = jnp.dot(a_ref[...], w_ref[...], preferred_element_type=F32)
    r = jnp.maximum(acc, 0.0)
    o_ref[...] = (r * r).astype(o_ref.dtype)


def _relu2_matmul(a, w, *, bm=1024, bn=1024):
    m, k = a.shape
    n = w.shape[1]
    return pl.pallas_call(
        _relu2_mm_kernel,
        grid=(m // bm, n // bn),
        in_specs=[pl.BlockSpec((bm, k), lambda i, j: (i, 0)),
                  pl.BlockSpec((k, bn), lambda i, j: (0, j))],
        out_specs=pl.BlockSpec((bm, bn), lambda i, j: (i, j)),
        out_shape=jax.ShapeDtypeStruct((m, n), BF16),
        compiler_params=_params("arbitrary", "arbitrary"),
        name="mlp_up",
    )(a, w)


def _residual_mm_ksplit_kernel(a_ref, w_ref, x_ref, g_ref, o_ref):
    kk = pl.program_id(2)

    @pl.when(kk == 0)
    def _():
        o_ref[...] = jnp.zeros_like(o_ref)

    o_ref[...] += jnp.dot(a_ref[...], w_ref[...], preferred_element_type=F32)

    @pl.when(kk == pl.num_programs(2) - 1)
    def _():
        o_ref[...] = x_ref[...] + g_ref[...] * o_ref[...]


def _residual_matmul_ksplit(a, w, x2d, gate, seq, *, bm=1024, bn=1024, bk=4096):
    m, k = a.shape
    n = w.shape[1]
    bsz = gate.shape[0]
    per_seq = seq // bm
    return pl.pallas_call(
        _residual_mm_ksplit_kernel,
        grid=(m // bm, n // bn, k // bk),
        in_specs=[pl.BlockSpec((bm, bk), lambda i, j, kk: (i, kk)),
                  pl.BlockSpec((bk, bn), lambda i, j, kk: (kk, j)),
                  pl.BlockSpec((bm, bn), lambda i, j, kk: (i, j)),
                  pl.BlockSpec((None, 1, bn), lambda i, j, kk: (i // per_seq, 0, j))],
        out_specs=pl.BlockSpec((bm, bn), lambda i, j, kk: (i, j)),
        out_shape=jax.ShapeDtypeStruct((m, n), F32),
        compiler_params=_params("arbitrary", "arbitrary", "arbitrary"),
        name="mlp_down",
    )(a, w, x2d, gate.reshape(bsz, 1, n))


def _suffix_sum_matrix():
    r = np.arange(2 * LANES)[:, None] % LANES
    c = np.arange(2 * LANES)[None, :]
    return jnp.asarray(np.where(c < LANES, r >= c, True), dtype=BF16)


def _sb_kernel(q_ref, k_ref, v_ref, ut_ref, o_ref, acc_ref, carry_ref, *, tq, tk, scale):
    seq = q_ref.shape[0]
    n_heads = q_ref.shape[1] // HEAD_DIM
    n_diag = tq // tk
    z_scale = scale * LOG2_E

    def visit(hd, q0, r0, nr, k0, triangle):
        hcols = slice(hd * HEAD_DIM, (hd + 1) * HEAD_DIM)
        rows = slice(r0, r0 + nr)
        q = q_ref[pl.ds(pl.multiple_of(q0 + r0, tk), nr), hcols]
        k = k_ref[pl.ds(k0, tk), hcols]
        v = v_ref[pl.ds(k0, tk), hcols]
        z2 = lax.dot_general(q, k, (((1,), (1,)), ((), ())), preferred_element_type=F32) * z_scale
        nlr = jnp.maximum(z2, 0.0) + jnp.log2(1.0 + jnp.exp2(-jnp.abs(z2)))
        if triangle:
            row = lax.broadcasted_iota(jnp.int32, (nr, tk), 0)
            col = lax.broadcasted_iota(jnp.int32, (nr, tk), 1)
            causal = col < row
            nlr = jnp.where(causal, nlr, 0.0)
        carry = carry_ref[hd, rows, :]
        parts = [None] * (tk // LANES)
        for sb in reversed(range(tk // LANES)):
            cols = slice(sb * LANES, (sb + 1) * LANES)
            x = nlr[:, cols]
            hi = x.astype(BF16)
            lo = (x - hi.astype(F32)).astype(BF16)
            cs = jnp.dot(jnp.concatenate([hi, lo], axis=1), ut_ref[...], preferred_element_type=F32)
            a = jnp.exp2(z2[:, cols] - cs[:, :LANES] - carry)
            if triangle:
                a = jnp.where(causal[:, cols], a, 0.0)
            parts[sb] = a.astype(BF16)
            carry = carry + cs[:, LANES:]
        carry_ref[hd, rows, :] = carry
        acc_ref[hd, rows, :] += jnp.dot(jnp.concatenate(parts, axis=1), v, preferred_element_type=F32)

    def q_block(iq, _):
        q0 = pl.multiple_of(iq * tq, tq)
        acc_ref[...] = jnp.zeros_like(acc_ref)
        carry_ref[...] = jnp.zeros_like(carry_ref)
        for d in reversed(range(n_diag)):
            for hd in range(n_heads):
                visit(hd, q0, d * tk, tk, q0 + d * tk, True)
                if d < n_diag - 1:
                    visit(hd, q0, (d + 1) * tk, tq - (d + 1) * tk, q0 + d * tk, False)

        def more(state):
            t, lowest = state
            return jnp.logical_and(t < iq * n_diag, lowest < SATURATED)

        def left(state):
            t, _ = state
            k0 = pl.multiple_of(q0 - (t + 1) * tk, tk)
            for hd in range(n_heads):
                visit(hd, q0, 0, tq, k0, False)
            return t + 1, jnp.min(carry_ref[...])

        lax.while_loop(more, left, (jnp.int32(0), jnp.min(carry_ref[...])))
        for hd in range(n_heads):
            o_ref[pl.ds(q0, tq), hd * HEAD_DIM:(hd + 1) * HEAD_DIM] = acc_ref[hd].astype(o_ref.dtype)
        return 0

    lax.fori_loop(0, seq // tq, q_block, 0)


def _stick_breaking(proj, bsz, seq, n_heads, q_col, k_col, v_col, *, tq=512, tk=256, heads_per_step=4):
    hps = min(heads_per_step, n_heads)
    wb = hps * HEAD_DIM
    qb, kb, vb = q_col // wb, k_col // wb, v_col // wb
    kern = functools.partial(_sb_kernel, tq=tq, tk=tk, scale=HEAD_DIM ** -0.5)
    return pl.pallas_call(
        kern,
        grid=(bsz, n_heads // hps),
        in_specs=[pl.BlockSpec((seq, wb), lambda b, h: (b, qb + h)),
                  pl.BlockSpec((seq, wb), lambda b, h: (b, kb + h)),
                  pl.BlockSpec((seq, wb), lambda b, h: (b, vb + h)),
                  pl.BlockSpec((2 * LANES, 2 * LANES), lambda b, h: (0, 0))],
        out_specs=pl.BlockSpec((seq, wb), lambda b, h: (b, h)),
        out_shape=jax.ShapeDtypeStruct((bsz * seq, n_heads * HEAD_DIM), BF16),
        scratch_shapes=[pltpu.VMEM((hps, tq, HEAD_DIM), F32), pltpu.VMEM((hps, tq, LANES), F32)],
        compiler_params=_params("arbitrary", "arbitrary"),
        name="stick_breaking",
    )(proj, proj, proj, _suffix_sum_matrix())


WINDOW = 128
DIL_TILE = 1024
MIN_PROBLEM_ROWS = 256


def _window_softmax(qs, ks, vs, valid, scale):
    scores = []
    for q, k in zip(qs, ks):
        s = lax.dot_general(q, k, (((1,), (1,)), ((), ())), preferred_element_type=F32) * scale
        scores.append(jnp.where(valid, s, -jnp.inf))
    stats = []
    for s in scores:
        m = jnp.max(s, axis=1, keepdims=True)
        p = jnp.exp(s - m)
        stats.append((p.astype(BF16), m, jnp.sum(p, axis=1, keepdims=True)))
    return [(jnp.dot(p, v, preferred_element_type=F32) / l, m + jnp.log(l))
            for (p, m, l), v in zip(stats, vs)]


def _dil_strided_kernel(q_ref, k_ref, v_ref, o_ref, lse_ref, qd_ref, kd_ref, vd_ref, *, dil, group, scale):
    n_heads, tile, _ = q_ref.shape
    chunk = tile // dil
    n_hist = -(-WINDOW // chunk)
    n_slots = n_hist + 1
    takes = [min(chunk, WINDOW - (back - 1) * chunk) for back in range(1, n_hist + 1)]
    assert group == 1 or all(t == chunk for t in takes)
    seg = group * chunk
    n_keys = group * (WINDOW + chunk)
    kt = pl.program_id(1)
    slot = lax.rem(kt, n_slots)

    @pl.when(kt == 0)
    def _():
        kd_ref[...] = jnp.zeros_like(kd_ref)
        vd_ref[...] = jnp.zeros_like(vd_ref)

    def deinterleave(c, _):
        rows = pl.ds(pl.multiple_of(c * chunk, chunk), chunk)
        for h in range(n_heads):
            qd_ref[h, rows, :] = q_ref[h, pl.ds(c, chunk, stride=dil), :].astype(BF16)
            kd_ref[slot, h, rows, :] = k_ref[h, pl.ds(c, chunk, stride=dil), :].astype(BF16)
            vd_ref[slot, h, rows, :] = v_ref[h, pl.ds(c, chunk, stride=dil), :].astype(BF16)
        return 0

    lax.fori_loop(0, dil, deinterleave, 0)

    row = lax.broadcasted_iota(jnp.int32, (seg, n_keys), 0)
    col = lax.broadcasted_iota(jnp.int32, (seg, n_keys), 1)
    if group == 1:
        i, u, same_class = row, col, None
    else:
        lc, ls = chunk.bit_length() - 1, seg.bit_length() - 1
        i = row & (chunk - 1)
        within = col & (seg - 1)
        u = ((col >> ls) << lc) + (within & (chunk - 1))
        same_class = (within >> lc) == (row >> lc)
    valid = jnp.logical_and(u >= i, u <= i + WINDOW)
    valid = jnp.logical_and(valid, u >= WINDOW - chunk * kt)
    if same_class is not None:
        valid = jnp.logical_and(valid, same_class)
    lane = lax.broadcasted_iota(jnp.int32, (seg, LANES), 1)

    def attend(gi, _):
        base = pl.multiple_of(gi * seg, seg)
        qs, ks, vs = [], [], []
        for h in range(n_heads):
            kparts, vparts = [], []
            for back in range(n_hist, 0, -1):
                take = takes[back - 1]
                old = lax.rem(kt + (n_slots - back), n_slots)
                rows = pl.ds(pl.multiple_of(base + (chunk - take), 16), take if group == 1 else seg)
                kparts.append(kd_ref[old, h, rows, :])
                vparts.append(vd_ref[old, h, rows, :])
            kparts.append(kd_ref[slot, h, pl.ds(base, seg), :])
            vparts.append(vd_ref[slot, h, pl.ds(base, seg), :])
            qs.append(qd_ref[h, pl.ds(base, seg), :])
            ks.append(jnp.concatenate(kparts, axis=0))
            vs.append(jnp.concatenate(vparts, axis=0))
        lse_blk = jnp.zeros((seg, LANES), F32)
        for h, (o, lse) in enumerate(_window_softmax(qs, ks, vs, valid, scale)):
            for cc in range(group):
                o_ref[h, pl.ds(gi * group + cc, chunk, stride=dil), :] = o[cc * chunk:(cc + 1) * chunk]
            lse_blk = jnp.where(lane == h, lse, lse_blk)
        for cc in range(group):
            lse_ref[pl.ds(gi * group + cc, chunk, stride=dil), :] = lse_blk[cc * chunk:(cc + 1) * chunk]
        return 0

    lax.fori_loop(0, dil // group, attend, 0)


def _dil_strided(heads_qkv, g, bsz, seq, n_heads):
    n = heads_qkv.shape[1]
    dil = DIL_GROUPS[g][1]
    tile = min(DIL_TILE, seq)
    chunk = tile // dil
    n_slots = -(-WINDOW // chunk) + 1
    group = max(1, min(dil, MIN_PROBLEM_ROWS // chunk))
    per_seq = seq // tile

    def operand(which):
        return pl.BlockSpec((n_heads, tile, HEAD_DIM), lambda b, t: (2 * which + g - 1, b * per_seq + t, 0))

    return pl.pallas_call(
        functools.partial(_dil_strided_kernel, dil=dil, group=group, scale=HEAD_DIM ** -0.5),
        grid=(bsz, per_seq),
        in_specs=[operand(0), operand(1), operand(2)],
        out_specs=[pl.BlockSpec((n_heads, tile, HEAD_DIM), lambda b, t: (0, b * per_seq + t, 0)),
                   pl.BlockSpec((tile, LANES), lambda b, t: (b * per_seq + t, 0))],
        out_shape=[jax.ShapeDtypeStruct((n_heads, n, HEAD_DIM), F32),
                   jax.ShapeDtypeStruct((n, LANES), F32)],
        scratch_shapes=[pltpu.VMEM((n_heads, tile, HEAD_DIM), BF16),
                        pltpu.VMEM((n_slots, n_heads, tile, HEAD_DIM), BF16),
                        pltpu.VMEM((n_slots, n_heads, tile, HEAD_DIM), BF16)],
        compiler_params=_params("arbitrary", "arbitrary"),
        name=f"dilated_stride{dil}",
    )(heads_qkv, heads_qkv, heads_qkv)


def _dil_merge_kernel(q_ref, kp_ref, kc_ref, vp_ref, vc_ref, o2_ref, l2_ref, o3_ref, l3_ref, y_ref,
                      *, scale):
    tile = q_ref.shape[0]
    n_heads = q_ref.shape[1] // HEAD_DIM
    blk = WINDOW
    first = pl.program_id(1) == 0
    row = lax.broadcasted_iota(jnp.int32, (blk, 2 * blk), 0)
    col = lax.broadcasted_iota(jnp.int32, (blk, 2 * blk), 1)
    band = jnp.logical_and(col >= row, col <= row + blk)
    band_first = jnp.logical_and(band, jnp.logical_or(col >= blk, jnp.logical_not(first)))
    for sub in range(tile // blk):
        rows = slice(sub * blk, (sub + 1) * blk)
        heads = [slice(h * HEAD_DIM, (h + 1) * HEAD_DIM) for h in range(n_heads)]
        if sub == 0:
            ks = [jnp.concatenate([kp_ref[:, cols], kc_ref[rows, cols]], axis=0) for cols in heads]
            vs = [jnp.concatenate([vp_ref[:, cols], vc_ref[rows, cols]], axis=0) for cols in heads]
            valid = band_first
        else:
            ks = [kc_ref[(sub - 1) * blk:(sub + 1) * blk, cols] for cols in heads]
            vs = [vc_ref[(sub - 1) * blk:(sub + 1) * blk, cols] for cols in heads]
            valid = band
        for h in range(n_heads):
            (o1, lse1), = _window_softmax([q_ref[rows, heads[h]]], ks[h:h + 1], vs[h:h + 1], valid, scale)
            lse2 = l2_ref[rows, h:h + 1]
            lse3 = l3_ref[rows, h:h + 1]
            mx = jnp.maximum(jnp.maximum(lse1, lse2), lse3)
            w1, w2, w3 = jnp.exp(lse1 - mx), jnp.exp(lse2 - mx), jnp.exp(lse3 - mx)
            y = (w1 * o1 + w2 * o2_ref[h, rows, :] + w3 * o3_ref[h, rows, :]) / (w1 + w2 + w3)
            y_ref[rows, heads[h]] = y.astype(y_ref.dtype)


def _dil_merge(proj, q_col, k_col, v_col, o2, l2, o3, l3, bsz, seq, n_heads, *, tile=512):
    n = proj.shape[0]
    gw = n_heads * HEAD_DIM
    tile = min(tile, seq)
    per_seq = seq // tile
    sub_per_tile = tile // WINDOW
    sub_per_seq = seq // WINDOW
    qo, ko, vo = q_col // gw, k_col // gw, v_col // gw

    def cur(off):
        return pl.BlockSpec((tile, gw), lambda b, t: (b * per_seq + t, off))

    def prev(off):
        return pl.BlockSpec(
            (WINDOW, gw), lambda b, t: (b * sub_per_seq + jnp.maximum(t * sub_per_tile - 1, 0), off))

    heads = pl.BlockSpec((n_heads, tile, HEAD_DIM), lambda b, t: (0, b * per_seq + t, 0))
    lses = pl.BlockSpec((tile, LANES), lambda b, t: (b * per_seq + t, 0))
    return pl.pallas_call(
        functools.partial(_dil_merge_kernel, scale=HEAD_DIM ** -0.5),
        grid=(bsz, per_seq),
        in_specs=[cur(qo), prev(ko), cur(ko), prev(vo), cur(vo), heads, lses, heads, lses],
        out_specs=pl.BlockSpec((tile, gw), lambda b, t: (b * per_seq + t, 0)),
        out_shape=jax.ShapeDtypeStruct((n, gw), BF16),
        compiler_params=_params("arbitrary", "arbitrary"),
        name="dilated_merge",
    )(proj, proj, proj, proj, proj, o2, l2, o3, l3)


def _dilated_attention(proj, q_col, k_col, v_col, heads_qkv, bsz, seq, n_heads):
    o3, l3 = _dil_strided(heads_qkv, 2, bsz, seq, n_heads)
    o2, l2 = _dil_strided(heads_qkv, 1, bsz, seq, n_heads)
    return _dil_merge(proj, q_col, k_col, v_col, o2, l2, o3, l3, bsz, seq, n_heads)


def kernel(x, c, positions, ada_w, ada_b, norm_mix_w, w_in, w_o_sb, w_o_dil, w_out,
           norm_mlp_w, w_ff1, w_ff2, norm_final_w):
    bsz, seq, d = x.shape
    n = bsz * seq
    d_sb = w_o_sb.shape[1]
    d_dil = w_o_dil.shape[1]
    sb_heads = d_sb // HEAD_DIM
    dil_heads = d_dil // HEAD_DIM
    bn = d_dil
    sb_t = 3 * d_sb // bn
    dil_t = lambda which, g: sb_t + which * N_DIL + g
    gate_t = sb_t + 3 * N_DIL
    main_tiles = (list(range(sb_t)) + [dil_t(0, 0), dil_t(1, 0), dil_t(2, 0)]
                  + list(range(gate_t, gate_t + 2 * d // bn)))
    main_rope = [0] * sb_t + [1, 1, 0] + [0] * (2 * d // bn)
    q0_col, k0_col, v0_col = sb_t * bn, (sb_t + 1) * bn, (sb_t + 2) * bn
    g_sb_col = (sb_t + 3) * bn
    g_dil_col = g_sb_col + d
    head_tiles = [dil_t(which, g) for which in range(3) for g in (1, 2)]
    head_rope = [1, 1, 1, 1, 0, 0]

    cos, sin = _rope_tables(positions)
    xf = x.reshape(n, d)
    for l in range(ada_w.shape[0]):
        mod = _ada_modulation(c, ada_w[l], ada_b[l])
        sh1, sc1, g1, sh2, sc2, g2 = [mod[:, i * d:(i + 1) * d] for i in range(6)]

        h = _norm_mod(xf, norm_mix_w[l], sc1, sh1, seq)
        w_in_l = w_in[l].astype(BF16)
        proj, (w_ff1_l, w_ff2_l) = _in_projection(
            h, w_in_l, cos, sin, main_tiles, main_rope, head_major=False, bn=bn,
            to_bf16=(w_ff1[l], w_ff2[l]))
        heads_qkv, _ = _in_projection(h, w_in_l, cos, sin, head_tiles, head_rope, head_major=True, bn=bn)
        y_a = _stick_breaking(proj, bsz, seq, sb_heads, 0, d_sb, 2 * d_sb)
        y_b = _dilated_attention(proj, q0_col, k0_col, v0_col, heads_qkv, bsz, seq, dil_heads)
        merged = _gated_merge(y_a, w_o_sb[l].astype(BF16), y_b, w_o_dil[l].astype(BF16),
                              proj, g_sb_col, g_dil_col)
        xf = _residual_matmul(merged, w_out[l].astype(BF16), xf, g1, seq)

        h = _norm_mod(xf, norm_mlp_w[l], sc2, sh2, seq)
        u = _relu2_matmul(h, w_ff1_l)
        xf = _residual_matmul_ksplit(u, w_ff2_l, xf, g2, seq)
    return _final_norm(xf, norm_final_w).reshape(bsz, seq, d)
```

```python
import functools

import jax
import jax.numpy as jnp
import numpy as np
from jax import lax
from jax.experimental import pallas as pl
from jax.experimental.pallas import tpu as pltpu

F32 = jnp.float32
BF16 = jnp.bfloat16

HEAD_DIM = 128
LANES = 128
BF16_ROWS = 16
DIL_GROUPS = ((128, 1), (512, 4), (2048, 16))
N_DIL = len(DIL_GROUPS)
ROPE_THETA = 10000.0
EPS = 1e-6
LOG2_E = 1.4426950408889634
SATURATED = 160.0
VMEM_LIMIT = 56 * 1024 * 1024


def _params(*sem):
    return pltpu.CompilerParams(dimension_semantics=sem, vmem_limit_bytes=VMEM_LIMIT)


def _ada_kernel(c_ref, w_ref, b_ref, o_ref):
    c = c_ref[...]
    s = c * (1.0 / (1.0 + jnp.exp(-c)))
    o_ref[...] = jnp.dot(s.astype(BF16), w_ref[...].astype(BF16),
                         preferred_element_type=F32) + b_ref[...]


def _ada_modulation(c, w, b, *, bn=512):
    bsz, d = c.shape
    n = w.shape[1]
    rows = 8
    c_pad = jnp.zeros((rows, d), F32).at[:bsz].set(c)
    out = pl.pallas_call(
        _ada_kernel,
        grid=(n // bn,),
        in_specs=[pl.BlockSpec((rows, d), lambda j: (0, 0)),
                  pl.BlockSpec((d, bn), lambda j: (0, j)),
                  pl.BlockSpec((1, bn), lambda j: (0, j))],
        out_specs=pl.BlockSpec((rows, bn), lambda j: (0, j)),
        out_shape=jax.ShapeDtypeStruct((rows, n), F32),
        compiler_params=_params("arbitrary"),
        name="ada_modulation",
    )(c_pad, w, b.reshape(1, n))
    return out[:bsz]


def _norm_mod_kernel(x_ref, w_ref, sc_ref, sh_ref, o_ref):
    x = x_ref[...]
    inv = lax.rsqrt(jnp.mean(x * x, axis=-1, keepdims=True) + EPS)
    mul = w_ref[...] * (1.0 + sc_ref[...])
    o_ref[...] = (x * inv * mul + sh_ref[...]).astype(o_ref.dtype)


def _norm_mod(x2d, w, sc, sh, seq, *, tm=512):
    n, d = x2d.shape
    bsz = sc.shape[0]
    per_seq = seq // tm
    return pl.pallas_call(
        _norm_mod_kernel,
        grid=(n // tm,),
        in_specs=[pl.BlockSpec((tm, d), lambda i: (i, 0)),
                  pl.BlockSpec((1, d), lambda i: (0, 0)),
                  pl.BlockSpec((None, 1, d), lambda i: (i // per_seq, 0, 0)),
                  pl.BlockSpec((None, 1, d), lambda i: (i // per_seq, 0, 0))],
        out_specs=pl.BlockSpec((tm, d), lambda i: (i, 0)),
        out_shape=jax.ShapeDtypeStruct((n, d), BF16),
        compiler_params=_params("arbitrary"),
        name="norm_modulate",
    )(x2d, w.reshape(1, d), sc.reshape(bsz, 1, d), sh.reshape(bsz, 1, d))


def _final_norm_kernel(x_ref, w_ref, o_ref):
    x = x_ref[...]
    inv = lax.rsqrt(jnp.mean(x * x, axis=-1, keepdims=True) + EPS)
    o_ref[...] = x * inv * w_ref[...]


def _final_norm(x2d, w, *, tm=512):
    n, d = x2d.shape
    return pl.pallas_call(
        _final_norm_kernel,
        grid=(n // tm,),
        in_specs=[pl.BlockSpec((tm, d), lambda i: (i, 0)),
                  pl.BlockSpec((1, d), lambda i: (0, 0))],
        out_specs=pl.BlockSpec((tm, d), lambda i: (i, 0)),
        out_shape=jax.ShapeDtypeStruct((n, d), F32),
        compiler_params=_params("arbitrary"),
        name="final_norm",
    )(x2d, w.reshape(1, d))


def _rope_table_kernel(pos_ref, invf_ref, sign_ref, cos_ref, sin_ref):
    ang = pos_ref[...].astype(F32) * invf_ref[...]
    cos_ref[...] = jnp.cos(ang)
    sin_ref[...] = jnp.sin(ang) * sign_ref[...]


def _rope_tables(positions, *, tm=2048):
    n = positions.size
    half = HEAD_DIM // 2
    inv_freq = ROPE_THETA ** (-jnp.arange(half, dtype=F32) / half)
    invf = jnp.concatenate([inv_freq, inv_freq]).reshape(1, HEAD_DIM)
    sign = jnp.concatenate([-jnp.ones((half,), F32), jnp.ones((half,), F32)]).reshape(1, HEAD_DIM)
    tm = min(tm, n)
    return pl.pallas_call(
        _rope_table_kernel,
        grid=(n // tm,),
        in_specs=[pl.BlockSpec((tm, 1), lambda i: (i, 0)),
                  pl.BlockSpec((1, HEAD_DIM), lambda i: (0, 0)),
                  pl.BlockSpec((1, HEAD_DIM), lambda i: (0, 0))],
        out_specs=[pl.BlockSpec((tm, HEAD_DIM), lambda i: (i, 0)),
                   pl.BlockSpec((tm, HEAD_DIM), lambda i: (i, 0))],
        out_shape=[jax.ShapeDtypeStruct((n, HEAD_DIM), F32),
                   jax.ShapeDtypeStruct((n, HEAD_DIM), F32)],
        compiler_params=_params("arbitrary"),
        name="rope_tables",
    )(positions.reshape(n, 1), invf, sign)


def _side_cast_spec(mat, n_steps, step_of):
    rows, cols = mat.shape
    blk = BF16_ROWS
    while rows % blk or rows // blk > n_steps:
        blk += BF16_ROWS
    last = rows // blk - 1
    return pl.BlockSpec((blk, cols), lambda *idx: (jnp.minimum(step_of(*idx), last), 0))


def _side_casts(refs, n_cast):
    for src_ref, dst_ref in zip(refs[:n_cast], refs[len(refs) - n_cast:]):
        dst_ref[...] = src_ref[...].astype(dst_ref.dtype)


def _inproj_kernel(tile_ref, a_ref, w_ref, *rest, head_major, rope, n_cast):
    del tile_ref
    if rope:
        cos, sin = rest[0][...], rest[1][...]
        rest = rest[2:]
    o_ref = rest[n_cast]
    _side_casts(rest, n_cast)
    acc = jnp.dot(a_ref[...], w_ref[...], preferred_element_type=F32)
    for h in range(acc.shape[1] // HEAD_DIM):
        val = acc[:, h * HEAD_DIM:(h + 1) * HEAD_DIM]
        if rope:
            val = val * cos + pltpu.roll(val, HEAD_DIM // 2, 1) * sin
        if head_major:
            o_ref[h] = val.astype(o_ref.dtype)
        else:
            o_ref[:, h * HEAD_DIM:(h + 1) * HEAD_DIM] = val.astype(o_ref.dtype)


def _in_projection(h, w, col_tiles, *, head_major, rope_tables=None, to_bf16=(), name, bm=1024, bn=1024):
    m, k = h.shape
    nt = len(col_tiles)
    n_steps = (m // bm) * nt
    if head_major:
        out_spec = pl.BlockSpec((bn // HEAD_DIM, bm, HEAD_DIM), lambda i, j, t: (j, i, 0))
        out_shape = jax.ShapeDtypeStruct((nt * (bn // HEAD_DIM), m, HEAD_DIM), F32)
    else:
        out_spec = pl.BlockSpec((bm, bn), lambda i, j, t: (i, j))
        out_shape = jax.ShapeDtypeStruct((m, nt * bn), BF16)

    cast_specs = [_side_cast_spec(mat, n_steps, lambda i, j, t: i * nt + j) for mat in to_bf16]

    rope_in = list(rope_tables or ())
    table_spec = pl.BlockSpec((bm, HEAD_DIM), lambda i, j, t: (i, 0))
    outs = pl.pallas_call(
        functools.partial(_inproj_kernel, head_major=head_major, rope=bool(rope_in), n_cast=len(to_bf16)),
        grid_spec=pltpu.PrefetchScalarGridSpec(
            num_scalar_prefetch=1,
            grid=(m // bm, nt),
            in_specs=[pl.BlockSpec((bm, k), lambda i, j, t: (i, 0)),
                      pl.BlockSpec((k, bn), lambda i, j, t: (0, t[j]))]
                     + [table_spec] * len(rope_in) + cast_specs,
            out_specs=[out_spec] + cast_specs),
        out_shape=[out_shape] + [jax.ShapeDtypeStruct(mat.shape, BF16) for mat in to_bf16],
        compiler_params=_params("arbitrary", "arbitrary"),
        name=name,
    )(jnp.asarray(col_tiles, jnp.int32), h, w, *rope_in, *to_bf16)
    return outs[0], outs[1:]


def _merge_kernel(ya_ref, wa_ref, yb_ref, wb_ref, ga_ref, gb_ref, o_ref):
    a = jnp.dot(ya_ref[...], wa_ref[...], preferred_element_type=F32)
    b = jnp.dot(yb_ref[...], wb_ref[...], preferred_element_type=F32)
    ga = ga_ref[...].astype(F32)
    gb = gb_ref[...].astype(F32)
    o_ref[...] = (a / (1.0 + jnp.exp(-ga)) + b / (1.0 + jnp.exp(-gb))).astype(o_ref.dtype)


def _gated_merge(ya, wa, yb, wb, proj, ga_col, gb_col, *, bm=1024, bn=1024):
    m = ya.shape[0]
    n = wa.shape[1]
    ga_blk, gb_blk = ga_col // bn, gb_col // bn
    return pl.pallas_call(
        _merge_kernel,
        grid=(m // bm, n // bn),
        in_specs=[pl.BlockSpec((bm, ya.shape[1]), lambda i, j: (i, 0)),
                  pl.BlockSpec((wa.shape[0], bn), lambda i, j: (0, j)),
                  pl.BlockSpec((bm, yb.shape[1]), lambda i, j: (i, 0)),
                  pl.BlockSpec((wb.shape[0], bn), lambda i, j: (0, j)),
                  pl.BlockSpec((bm, bn), lambda i, j: (i, ga_blk + j)),
                  pl.BlockSpec((bm, bn), lambda i, j: (i, gb_blk + j))],
        out_specs=pl.BlockSpec((bm, bn), lambda i, j: (i, j)),
        out_shape=jax.ShapeDtypeStruct((m, n), BF16),
        compiler_params=_params("arbitrary", "arbitrary"),
        name="gated_merge",
    )(ya, wa, yb, wb, proj, proj)


def _residual_mm_kernel(a_ref, w_ref, x_ref, g_ref, o_ref):
    acc = jnp.dot(a_ref[...], w_ref[...], preferred_element_type=F32)
    o_ref[...] = x_ref[...] + g_ref[...] * acc


def _residual_matmul(a, w, x2d, gate, seq, *, bm=1024, bn=1024):
    m, k = a.shape
    n = w.shape[1]
    bsz = gate.shape[0]
    per_seq = seq // bm
    return pl.pallas_call(
        _residual_mm_kernel,
        grid=(m // bm, n // bn),
        in_specs=[pl.BlockSpec((bm, k), lambda i, j: (i, 0)),
                  pl.BlockSpec((k, bn), lambda i, j: (0, j)),
                  pl.BlockSpec((bm, bn), lambda i, j: (i, j)),
                  pl.BlockSpec((None, 1, bn), lambda i, j: (i // per_seq, 0, j))],
        out_specs=pl.BlockSpec((bm, bn), lambda i, j: (i, j)),
        out_shape=jax.ShapeDtypeStruct((m, n), F32),
        compiler_params=_params("arbitrary", "arbitrary"),
        name="out_projection",
    )(a, w, x2d, gate.reshape(bsz, 1, n))


def _relu2_mm_kernel(a_ref, w_ref, *rest, n_cast):
    o_ref = rest[n_cast]
    _side_casts(rest, n_cast)
    acc = jnp.dot(a_ref[...], w_ref[...], preferred_element_type=F32)
    r = jnp.maximum(acc, 0.0)
    o_ref[...] = (r * r).astype(o_ref.dtype)


def _relu2_matmul(a, w, *, to_bf16=(), bm=1024, bn=1024):
    m, k = a.shape
    n = w.shape[1]
    nj = n // bn
    cast_specs = [_side_cast_spec(mat, (m // bm) * nj, lambda i, j: i * nj + j) for mat in to_bf16]
    outs = pl.pallas_call(
        functools.partial(_relu2_mm_kernel, n_cast=len(to_bf16)),
        grid=(m // bm, nj),
        in_specs=[pl.BlockSpec((bm, k), lambda i, j: (i, 0)),
                  pl.BlockSpec((k, bn), lambda i, j: (0, j))] + cast_specs,
        out_specs=[pl.BlockSpec((bm, bn), lambda i, j: (i, j))] + cast_specs,
        out_shape=[jax.ShapeDtypeStruct((m, n), BF16)]
                  + [jax.ShapeDtypeStruct(mat.shape, BF16) for mat in to_bf16],
        compiler_params=_params("arbitrary", "arbitrary"),
        name="mlp_up",
    )(a, w, *to_bf16)
    return outs[0], outs[1:]


def _residual_mm_ksplit_kernel(a_ref, w_ref, x_ref, g_ref, o_ref):
    kk = pl.program_id(2)

    @pl.when(kk == 0)
    def _():
        o_ref[...] = jnp.zeros_like(o_ref)

    o_ref[...] += jnp.dot(a_ref[...], w_ref[...], preferred_element_type=F32)

    @pl.when(kk == pl.num_programs(2) - 1)
    def _():
        o_ref[...] = x_ref[...] + g_ref[...] * o_ref[...]


def _residual_matmul_ksplit(a, w, x2d, gate, seq, *, bm=1024, bn=1024, bk=4096):
    m, k = a.shape
    n = w.shape[1]
    bsz = gate.shape[0]
    per_seq = seq // bm
    return pl.pallas_call(
        _residual_mm_ksplit_kernel,
        grid=(m // bm, n // bn, k // bk),
        in_specs=[pl.BlockSpec((bm, bk), lambda i, j, kk: (i, kk)),
                  pl.BlockSpec((bk, bn), lambda i, j, kk: (kk, j)),
                  pl.BlockSpec((bm, bn), lambda i, j, kk: (i, j)),
                  pl.BlockSpec((None, 1, bn), lambda i, j, kk: (i // per_seq, 0, j))],
        out_specs=pl.BlockSpec((bm, bn), lambda i, j, kk: (i, j)),
        out_shape=jax.ShapeDtypeStruct((m, n), F32),
        compiler_params=_params("arbitrary", "arbitrary", "arbitrary"),
        name="mlp_down",
    )(a, w, x2d, gate.reshape(bsz, 1, n))


def _suffix_sum_matrix():
    r = np.arange(2 * LANES)[:, None] % LANES
    c = np.arange(2 * LANES)[None, :]
    return jnp.asarray(np.where(c < LANES, r >= c, True), dtype=BF16)


def _sb_kernel(q_ref, k_ref, v_ref, ut_ref, o_ref, acc_ref, carry_ref, *, tq, tk, scale):
    seq = q_ref.shape[0]
    n_heads = q_ref.shape[1] // HEAD_DIM
    n_diag = tq // tk
    z_scale = scale * LOG2_E

    def visit(hd, q0, r0, nr, k0, triangle):
        hcols = slice(hd * HEAD_DIM, (hd + 1) * HEAD_DIM)
        rows = slice(r0, r0 + nr)
        q = q_ref[pl.ds(pl.multiple_of(q0 + r0, tk), nr), hcols]
        k = k_ref[pl.ds(k0, tk), hcols]
        v = v_ref[pl.ds(k0, tk), hcols]
        z2 = lax.dot_general(q, k, (((1,), (1,)), ((), ())), preferred_element_type=F32) * z_scale
        nlr = jnp.maximum(z2, 0.0) + jnp.log2(1.0 + jnp.exp2(-jnp.abs(z2)))
        if triangle:
            row = lax.broadcasted_iota(jnp.int32, (nr, tk), 0)
            col = lax.broadcasted_iota(jnp.int32, (nr, tk), 1)
            causal = col < row
            nlr = jnp.where(causal, nlr, 0.0)
        carry = carry_ref[hd, rows, :]
        parts = [None] * (tk // LANES)
        for sb in reversed(range(tk // LANES)):
            cols = slice(sb * LANES, (sb + 1) * LANES)
            x = nlr[:, cols]
            hi = x.astype(BF16)
            lo = (x - hi.astype(F32)).astype(BF16)
            cs = jnp.dot(jnp.concatenate([hi, lo], axis=1), ut_ref[...], preferred_element_type=F32)
            a = jnp.exp2(z2[:, cols] - cs[:, :LANES] - carry)
            if triangle:
                a = jnp.where(causal[:, cols], a, 0.0)
            parts[sb] = a.astype(BF16)
            carry = carry + cs[:, LANES:]
        carry_ref[hd, rows, :] = carry
        acc_ref[hd, rows, :] += jnp.dot(jnp.concatenate(parts, axis=1), v, preferred_element_type=F32)

    def q_block(iq, _):
        q0 = pl.multiple_of(iq * tq, tq)
        acc_ref[...] = jnp.zeros_like(acc_ref)
        carry_ref[...] = jnp.zeros_like(carry_ref)
        for d in reversed(range(n_diag)):
            for hd in range(n_heads):
                visit(hd, q0, d * tk, tk, q0 + d * tk, True)
                if d < n_diag - 1:
                    visit(hd, q0, (d + 1) * tk, tq - (d + 1) * tk, q0 + d * tk, False)

        def more(state):
            t, lowest = state
            return jnp.logical_and(t < iq * n_diag, lowest < SATURATED)

        def left(state):
            t, _ = state
            k0 = pl.multiple_of(q0 - (t + 1) * tk, tk)
            for hd in range(n_heads):
                visit(hd, q0, 0, tq, k0, False)
            return t + 1, jnp.min(carry_ref[...])

        lax.while_loop(more, left, (jnp.int32(0), jnp.min(carry_ref[...])))
        for hd in range(n_heads):
            o_ref[pl.ds(q0, tq), hd * HEAD_DIM:(hd + 1) * HEAD_DIM] = acc_ref[hd].astype(o_ref.dtype)
        return 0

    lax.fori_loop(0, seq // tq, q_block, 0)


def _stick_breaking(proj, bsz, seq, n_heads, q_col, k_col, v_col, *, tq=512, tk=256, heads_per_step=4):
    hps = min(heads_per_step, n_heads)
    wb = hps * HEAD_DIM
    qb, kb, vb = q_col // wb, k_col // wb, v_col // wb
    kern = functools.partial(_sb_kernel, tq=tq, tk=tk, scale=HEAD_DIM ** -0.5)
    return pl.pallas_call(
        kern,
        grid=(bsz, n_heads // hps),
        in_specs=[pl.BlockSpec((seq, wb), lambda b, h: (b, qb + h)),
                  pl.BlockSpec((seq, wb), lambda b, h: (b, kb + h)),
                  pl.BlockSpec((seq, wb), lambda b, h: (b, vb + h)),
                  pl.BlockSpec((2 * LANES, 2 * LANES), lambda b, h: (0, 0))],
        out_specs=pl.BlockSpec((seq, wb), lambda b, h: (b, h)),
        out_shape=jax.ShapeDtypeStruct((bsz * seq, n_heads * HEAD_DIM), BF16),
        scratch_shapes=[pltpu.VMEM((hps, tq, HEAD_DIM), F32), pltpu.VMEM((hps, tq, LANES), F32)],
        compiler_params=_params("arbitrary", "arbitrary"),
        name="stick_breaking",
    )(proj, proj, proj, _suffix_sum_matrix())


WINDOW = 128
DIL_TILE = 1024
MIN_PROBLEM_ROWS = 256


def _window_softmax(qs, ks, vs, valid, scale):
    scores = []
    for q, k in zip(qs, ks):
        s = lax.dot_general(q, k, (((1,), (1,)), ((), ())), preferred_element_type=F32) * scale
        scores.append(jnp.where(valid, s, -jnp.inf))
    stats = []
    for s in scores:
        m = jnp.max(s, axis=1, keepdims=True)
        p = jnp.exp(s - m)
        stats.append((p.astype(BF16), m, jnp.sum(p, axis=1, keepdims=True)))
    return [(jnp.dot(p, v, preferred_element_type=F32) / l, m + jnp.log(l))
            for (p, m, l), v in zip(stats, vs)]


def _dil_strided_kernel(q_ref, k_ref, v_ref, o_ref, lse_ref, qd_ref, kd_ref, vd_ref, *, dil, group, scale):
    n_heads, tile, _ = q_ref.shape
    chunk = tile // dil
    n_hist = -(-WINDOW // chunk)
    n_slots = n_hist + 1
    takes = [min(chunk, WINDOW - (back - 1) * chunk) for back in range(1, n_hist + 1)]
    assert group == 1 or all(t == chunk for t in takes)
    seg = group * chunk
    n_keys = group * (WINDOW + chunk)
    kt = pl.program_id(1)
    slot = lax.rem(kt, n_slots)

    @pl.when(kt == 0)
    def _():
        kd_ref[...] = jnp.zeros_like(kd_ref)
        vd_ref[...] = jnp.zeros_like(vd_ref)

    def deinterleave(c, _):
        rows = pl.ds(pl.multiple_of(c * chunk, chunk), chunk)
        for h in range(n_heads):
            qd_ref[h, rows, :] = q_ref[h, pl.ds(c, chunk, stride=dil), :].astype(BF16)
            kd_ref[slot, h, rows, :] = k_ref[h, pl.ds(c, chunk, stride=dil), :].astype(BF16)
            vd_ref[slot, h, rows, :] = v_ref[h, pl.ds(c, chunk, stride=dil), :].astype(BF16)
        return 0

    lax.fori_loop(0, dil, deinterleave, 0)

    row = lax.broadcasted_iota(jnp.int32, (seg, n_keys), 0)
    col = lax.broadcasted_iota(jnp.int32, (seg, n_keys), 1)
    if group == 1:
        i, u, same_class = row, col, None
    else:
        lc, ls = chunk.bit_length() - 1, seg.bit_length() - 1
        i = row & (chunk - 1)
        within = col & (seg - 1)
        u = ((col >> ls) << lc) + (within & (chunk - 1))
        same_class = (within >> lc) == (row >> lc)
    valid = jnp.logical_and(u >= i, u <= i + WINDOW)
    valid = jnp.logical_and(valid, u >= WINDOW - chunk * kt)
    if same_class is not None:
        valid = jnp.logical_and(valid, same_class)
    lane = lax.broadcasted_iota(jnp.int32, (seg, LANES), 1)

    def attend(gi, _):
        base = pl.multiple_of(gi * seg, seg)
        qs, ks, vs = [], [], []
        for h in range(n_heads):
            kparts, vparts = [], []
            for back in range(n_hist, 0, -1):
                take = takes[back - 1]
                old = lax.rem(kt + (n_slots - back), n_slots)
                rows = pl.ds(pl.multiple_of(base + (chunk - take), 16), take if group == 1 else seg)
                kparts.append(kd_ref[old, h, rows, :])
                vparts.append(vd_ref[old, h, rows, :])
            kparts.append(kd_ref[slot, h, pl.ds(base, seg), :])
            vparts.append(vd_ref[slot, h, pl.ds(base, seg), :])
            qs.append(qd_ref[h, pl.ds(base, seg), :])
            ks.append(jnp.concatenate(kparts, axis=0))
            vs.append(jnp.concatenate(vparts, axis=0))
        lse_blk = jnp.zeros((seg, LANES), F32)
        for h, (o, lse) in enumerate(_window_softmax(qs, ks, vs, valid, scale)):
            for cc in range(group):
                o_ref[h, pl.ds(gi * group + cc, chunk, stride=dil), :] = o[cc * chunk:(cc + 1) * chunk]
            lse_blk = jnp.where(lane == h, lse, lse_blk)
        for cc in range(group):
            lse_ref[pl.ds(gi * group + cc, chunk, stride=dil), :] = lse_blk[cc * chunk:(cc + 1) * chunk]
        return 0

    lax.fori_loop(0, dil // group, attend, 0)


def _dil_strided(heads_qk, heads_v, g, bsz, seq, n_heads):
    n = heads_qk.shape[1]
    dil = DIL_GROUPS[g][1]
    tile = min(DIL_TILE, seq)
    chunk = tile // dil
    n_slots = -(-WINDOW // chunk) + 1
    group = max(1, min(dil, MIN_PROBLEM_ROWS // chunk))
    per_seq = seq // tile

    def operand(slab_block):
        return pl.BlockSpec((n_heads, tile, HEAD_DIM), lambda b, t: (slab_block, b * per_seq + t, 0))

    return pl.pallas_call(
        functools.partial(_dil_strided_kernel, dil=dil, group=group, scale=HEAD_DIM ** -0.5),
        grid=(bsz, per_seq),
        in_specs=[operand(g - 1), operand(2 + g - 1), operand(g - 1)],
        out_specs=[pl.BlockSpec((n_heads, tile, HEAD_DIM), lambda b, t: (0, b * per_seq + t, 0)),
                   pl.BlockSpec((tile, LANES), lambda b, t: (b * per_seq + t, 0))],
        out_shape=[jax.ShapeDtypeStruct((n_heads, n, HEAD_DIM), F32),
                   jax.ShapeDtypeStruct((n, LANES), F32)],
        scratch_shapes=[pltpu.VMEM((n_heads, tile, HEAD_DIM), BF16),
                        pltpu.VMEM((n_slots, n_heads, tile, HEAD_DIM), BF16),
                        pltpu.VMEM((n_slots, n_heads, tile, HEAD_DIM), BF16)],
        compiler_params=_params("arbitrary", "arbitrary"),
        name=f"dilated_stride{dil}",
    )(heads_qk, heads_qk, heads_v)


def _dil_merge_kernel(q_ref, kp_ref, kc_ref, vp_ref, vc_ref, o2_ref, l2_ref, o3_ref, l3_ref, y_ref,
                      *, scale):
    tile = q_ref.shape[0]
    n_heads = q_ref.shape[1] // HEAD_DIM
    blk = WINDOW
    first = pl.program_id(1) == 0
    row = lax.broadcasted_iota(jnp.int32, (blk, 2 * blk), 0)
    col = lax.broadcasted_iota(jnp.int32, (blk, 2 * blk), 1)
    band = jnp.logical_and(col >= row, col <= row + blk)
    band_first = jnp.logical_and(band, jnp.logical_or(col >= blk, jnp.logical_not(first)))
    for sub in range(tile // blk):
        rows = slice(sub * blk, (sub + 1) * blk)
        heads = [slice(h * HEAD_DIM, (h + 1) * HEAD_DIM) for h in range(n_heads)]
        if sub == 0:
            ks = [jnp.concatenate([kp_ref[:, cols], kc_ref[rows, cols]], axis=0) for cols in heads]
            vs = [jnp.concatenate([vp_ref[:, cols], vc_ref[rows, cols]], axis=0) for cols in heads]
            valid = band_first
        else:
            ks = [kc_ref[(sub - 1) * blk:(sub + 1) * blk, cols] for cols in heads]
            vs = [vc_ref[(sub - 1) * blk:(sub + 1) * blk, cols] for cols in heads]
            valid = band
        for h in range(n_heads):
            (o1, lse1), = _window_softmax([q_ref[rows, heads[h]]], ks[h:h + 1], vs[h:h + 1], valid, scale)
            lse2 = l2_ref[rows, h:h + 1]
            lse3 = l3_ref[rows, h:h + 1]
            mx = jnp.maximum(jnp.maximum(lse1, lse2), lse3)
            w1, w2, w3 = jnp.exp(lse1 - mx), jnp.exp(lse2 - mx), jnp.exp(lse3 - mx)
            y = (w1 * o1 + w2 * o2_ref[h, rows, :] + w3 * o3_ref[h, rows, :]) / (w1 + w2 + w3)
            y_ref[rows, heads[h]] = y.astype(y_ref.dtype)


def _dil_merge(qk, q_col, k_col, proj, v_col, o2, l2, o3, l3, bsz, seq, n_heads, *, tile=512):
    n = proj.shape[0]
    gw = n_heads * HEAD_DIM
    tile = min(tile, seq)
    per_seq = seq // tile
    sub_per_tile = tile // WINDOW
    sub_per_seq = seq // WINDOW
    qo, ko, vo = q_col // gw, k_col // gw, v_col // gw

    def cur(off):
        return pl.BlockSpec((tile, gw), lambda b, t: (b * per_seq + t, off))

    def prev(off):
        return pl.BlockSpec(
            (WINDOW, gw), lambda b, t: (b * sub_per_seq + jnp.maximum(t * sub_per_tile - 1, 0), off))

    heads = pl.BlockSpec((n_heads, tile, HEAD_DIM), lambda b, t: (0, b * per_seq + t, 0))
    lses = pl.BlockSpec((tile, LANES), lambda b, t: (b * per_seq + t, 0))
    return pl.pallas_call(
        functools.partial(_dil_merge_kernel, scale=HEAD_DIM ** -0.5),
        grid=(bsz, per_seq),
        in_specs=[cur(qo), prev(ko), cur(ko), prev(vo), cur(vo), heads, lses, heads, lses],
        out_specs=pl.BlockSpec((tile, gw), lambda b, t: (b * per_seq + t, 0)),
        out_shape=jax.ShapeDtypeStruct((n, gw), BF16),
        compiler_params=_params("arbitrary", "arbitrary"),
        name="dilated_merge",
    )(qk, qk, qk, proj, proj, o2, l2, o3, l3)


def _dilated_attention(qk, q_col, k_col, proj, v_col, heads_qk, heads_v, bsz, seq, n_heads):
    o3, l3 = _dil_strided(heads_qk, heads_v, 2, bsz, seq, n_heads)
    o2, l2 = _dil_strided(heads_qk, heads_v, 1, bsz, seq, n_heads)
    return _dil_merge(qk, q_col, k_col, proj, v_col, o2, l2, o3, l3, bsz, seq, n_heads)


def kernel(x, c, positions, ada_w, ada_b, norm_mix_w, w_in, w_o_sb, w_o_dil, w_out,
           norm_mlp_w, w_ff1, w_ff2, norm_final_w):
    bsz, seq, d = x.shape
    n = bsz * seq
    d_sb = w_o_sb.shape[1]
    d_dil = w_o_dil.shape[1]
    sb_heads = d_sb // HEAD_DIM
    dil_heads = d_dil // HEAD_DIM
    bn = d_dil
    sb_t = 3 * d_sb // bn
    dil_t = lambda which, g: sb_t + which * N_DIL + g
    gate_t = sb_t + 3 * N_DIL
    plain_tiles = list(range(sb_t)) + [dil_t(2, 0)] + list(range(gate_t, gate_t + 2 * d // bn))
    v0_col = sb_t * bn
    g_sb_col = v0_col + bn
    g_dil_col = g_sb_col + d
    qk0_tiles = [dil_t(0, 0), dil_t(1, 0)]
    heads_qk_tiles = [dil_t(which, g) for which in range(2) for g in (1, 2)]
    heads_v_tiles = [dil_t(2, g) for g in (1, 2)]

    cos, sin = _rope_tables(positions)
    xf = x.reshape(n, d)
    for l in range(ada_w.shape[0]):
        mod = _ada_modulation(c, ada_w[l], ada_b[l])
        sh1, sc1, g1, sh2, sc2, g2 = [mod[:, i * d:(i + 1) * d] for i in range(6)]

        h = _norm_mod(xf, norm_mix_w[l], sc1, sh1, seq)
        w_in_l = w_in[l].astype(BF16)
        proj, (w_ff1_l,) = _in_projection(h, w_in_l, plain_tiles, head_major=False, bn=bn,
                                          to_bf16=(w_ff1[l],), name="in_projection")
        qk0, _ = _in_projection(h, w_in_l, qk0_tiles, head_major=False, rope_tables=(cos, sin), bn=bn,
                                name="in_projection_rotary")
        heads_qk, _ = _in_projection(h, w_in_l, heads_qk_tiles, head_major=True, rope_tables=(cos, sin),
                                     bn=bn, name="in_projection_heads_rotary")
        heads_v, _ = _in_projection(h, w_in_l, heads_v_tiles, head_major=True, bn=bn,
                                    name="in_projection_heads")
        y_a = _stick_breaking(proj, bsz, seq, sb_heads, 0, d_sb, 2 * d_sb)
        y_b = _dilated_attention(qk0, 0, bn, proj, v0_col, heads_qk, heads_v, bsz, seq, dil_heads)
        merged = _gated_merge(y_a, w_o_sb[l].astype(BF16), y_b, w_o_dil[l].astype(BF16),
                              proj, g_sb_col, g_dil_col)
        xf = _residual_matmul(merged, w_out[l].astype(BF16), xf, g1, seq)

        h = _norm_mod(xf, norm_mlp_w[l], sc2, sh2, seq)
        u, (w_ff2_l,) = _relu2_matmul(h, w_ff1_l, to_bf16=(w_ff2[l],))
        xf = _residual_matmul_ksplit(u, w_ff2_l, xf, g2, seq)
    return _final_norm(xf, norm_final_w).reshape(bsz, seq, d)
```

```python
import functools

import jax
import jax.numpy as jnp
import numpy as np
from jax import lax
from jax.experimental import pallas as pl
from jax.experimental.pallas import tpu as pltpu

F32 = jnp.float32
BF16 = jnp.bfloat16

HEAD_DIM = 128
LANES = 128
BF16_ROWS = 16
DIL_GROUPS = ((128, 1), (512, 4), (2048, 16))
N_DIL = len(DIL_GROUPS)
ROPE_THETA = 10000.0
EPS = 1e-6
LOG2_E = 1.4426950408889634
SATURATED = 160.0
VMEM_LIMIT = 56 * 1024 * 1024


def _params(*sem):
    return pltpu.CompilerParams(dimension_semantics=sem, vmem_limit_bytes=VMEM_LIMIT)


def _ada_kernel(c_ref, w_ref, b_ref, o_ref):
    c = c_ref[...]
    s = c * (1.0 / (1.0 + jnp.exp(-c)))
    o_ref[...] = jnp.dot(s.astype(BF16), w_ref[...].astype(BF16),
                         preferred_element_type=F32) + b_ref[...]


def _ada_modulation(c, w, b, *, bn=512):
    bsz, d = c.shape
    n = w.shape[1]
    rows = 8
    c_pad = jnp.zeros((rows, d), F32).at[:bsz].set(c)
    out = pl.pallas_call(
        _ada_kernel,
        grid=(n // bn,),
        in_specs=[pl.BlockSpec((rows, d), lambda j: (0, 0)),
                  pl.BlockSpec((d, bn), lambda j: (0, j)),
                  pl.BlockSpec((1, bn), lambda j: (0, j))],
        out_specs=pl.BlockSpec((rows, bn), lambda j: (0, j)),
        out_shape=jax.ShapeDtypeStruct((rows, n), F32),
        compiler_params=_params("arbitrary"),
        name="ada_modulation",
    )(c_pad, w, b.reshape(1, n))
    return out[:bsz]


def _norm_mod_kernel(x_ref, w_ref, sc_ref, sh_ref, o_ref):
    x = x_ref[...]
    inv = lax.rsqrt(jnp.mean(x * x, axis=-1, keepdims=True) + EPS)
    mul = w_ref[...] * (1.0 + sc_ref[...])
    o_ref[...] = (x * inv * mul + sh_ref[...]).astype(o_ref.dtype)


def _norm_mod(x2d, w, sc, sh, seq, *, tm=512):
    n, d = x2d.shape
    bsz = sc.shape[0]
    per_seq = seq // tm
    return pl.pallas_call(
        _norm_mod_kernel,
        grid=(n // tm,),
        in_specs=[pl.BlockSpec((tm, d), lambda i: (i, 0)),
                  pl.BlockSpec((1, d), lambda i: (0, 0)),
                  pl.BlockSpec((None, 1, d), lambda i: (i // per_seq, 0, 0)),
                  pl.BlockSpec((None, 1, d), lambda i: (i // per_seq, 0, 0))],
        out_specs=pl.BlockSpec((tm, d), lambda i: (i, 0)),
        out_shape=jax.ShapeDtypeStruct((n, d), BF16),
        compiler_params=_params("arbitrary"),
        name="norm_modulate",
    )(x2d, w.reshape(1, d), sc.reshape(bsz, 1, d), sh.reshape(bsz, 1, d))


def _final_norm_kernel(x_ref, w_ref, o_ref):
    x = x_ref[...]
    inv = lax.rsqrt(jnp.mean(x * x, axis=-1, keepdims=True) + EPS)
    o_ref[...] = x * inv * w_ref[...]


def _final_norm(x2d, w, *, tm=512):
    n, d = x2d.shape
    return pl.pallas_call(
        _final_norm_kernel,
        grid=(n // tm,),
        in_specs=[pl.BlockSpec((tm, d), lambda i: (i, 0)),
                  pl.BlockSpec((1, d), lambda i: (0, 0))],
        out_specs=pl.BlockSpec((tm, d), lambda i: (i, 0)),
        out_shape=jax.ShapeDtypeStruct((n, d), F32),
        compiler_params=_params("arbitrary"),
        name="final_norm",
    )(x2d, w.reshape(1, d))


def _rope_table_kernel(pos_ref, invf_ref, sign_ref, cos_ref, sin_ref):
    ang = pos_ref[...].astype(F32) * invf_ref[...]
    cos_ref[...] = jnp.cos(ang)
    sin_ref[...] = jnp.sin(ang) * sign_ref[...]


def _rope_tables(positions, *, tm=2048):
    n = positions.size
    half = HEAD_DIM // 2
    inv_freq = ROPE_THETA ** (-jnp.arange(half, dtype=F32) / half)
    invf = jnp.concatenate([inv_freq, inv_freq]).reshape(1, HEAD_DIM)
    sign = jnp.concatenate([-jnp.ones((half,), F32), jnp.ones((half,), F32)]).reshape(1, HEAD_DIM)
    tm = min(tm, n)
    return pl.pallas_call(
        _rope_table_kernel,
        grid=(n // tm,),
        in_specs=[pl.BlockSpec((tm, 1), lambda i: (i, 0)),
                  pl.BlockSpec((1, HEAD_DIM), lambda i: (0, 0)),
                  pl.BlockSpec((1, HEAD_DIM), lambda i: (0, 0))],
        out_specs=[pl.BlockSpec((tm, HEAD_DIM), lambda i: (i, 0)),
                   pl.BlockSpec((tm, HEAD_DIM), lambda i: (i, 0))],
        out_shape=[jax.ShapeDtypeStruct((n, HEAD_DIM), F32),
                   jax.ShapeDtypeStruct((n, HEAD_DIM), F32)],
        compiler_params=_params("arbitrary"),
        name="rope_tables",
    )(positions.reshape(n, 1), invf, sign)


def _side_cast_spec(mat, n_steps, step_of):
    rows, cols = mat.shape
    blk = BF16_ROWS
    while rows % blk or rows // blk > n_steps:
        blk += BF16_ROWS
    last = rows // blk - 1
    return pl.BlockSpec((blk, cols), lambda *idx: (jnp.minimum(step_of(*idx), last), 0))


def _side_casts(refs, n_cast):
    for src_ref, dst_ref in zip(refs[:n_cast], refs[len(refs) - n_cast:]):
        dst_ref[...] = src_ref[...].astype(dst_ref.dtype)


def _inproj_kernel(tile_ref, a_ref, w_ref, *rest, head_major, rope, n_cast):
    del tile_ref
    if rope:
        cos, sin = rest[0][...], rest[1][...]
        rest = rest[2:]
    o_ref = rest[n_cast]
    _side_casts(rest, n_cast)
    acc = jnp.dot(a_ref[...], w_ref[...], preferred_element_type=F32)
    for h in range(acc.shape[1] // HEAD_DIM):
        val = acc[:, h * HEAD_DIM:(h + 1) * HEAD_DIM]
        if rope:
            val = val * cos + pltpu.roll(val, HEAD_DIM // 2, 1) * sin
        if head_major:
            o_ref[h] = val.astype(o_ref.dtype)
        else:
            o_ref[:, h * HEAD_DIM:(h + 1) * HEAD_DIM] = val.astype(o_ref.dtype)


def _in_projection(h, w, col_tiles, *, head_major, rope_tables=None, to_bf16=(), name, bm=1024, bn=1024):
    m, k = h.shape
    nt = len(col_tiles)
    n_steps = (m // bm) * nt
    if head_major:
        out_spec = pl.BlockSpec((bn // HEAD_DIM, bm, HEAD_DIM), lambda i, j, t: (j, i, 0))
        out_shape = jax.ShapeDtypeStruct((nt * (bn // HEAD_DIM), m, HEAD_DIM), F32)
    else:
        out_spec = pl.BlockSpec((bm, bn), lambda i, j, t: (i, j))
        out_shape = jax.ShapeDtypeStruct((m, nt * bn), BF16)

    cast_specs = [_side_cast_spec(mat, n_steps, lambda i, j, t: i * nt + j) for mat in to_bf16]

    rope_in = list(rope_tables or ())
    table_spec = pl.BlockSpec((bm, HEAD_DIM), lambda i, j, t: (i, 0))
    outs = pl.pallas_call(
        functools.partial(_inproj_kernel, head_major=head_major, rope=bool(rope_in), n_cast=len(to_bf16)),
        grid_spec=pltpu.PrefetchScalarGridSpec(
            num_scalar_prefetch=1,
            grid=(m // bm, nt),
            in_specs=[pl.BlockSpec((bm, k), lambda i, j, t: (i, 0)),
                      pl.BlockSpec((k, bn), lambda i, j, t: (0, t[j]))]
                     + [table_spec] * len(rope_in) + cast_specs,
            out_specs=[out_spec] + cast_specs),
        out_shape=[out_shape] + [jax.ShapeDtypeStruct(mat.shape, BF16) for mat in to_bf16],
        compiler_params=_params("arbitrary", "arbitrary"),
        name=name,
    )(jnp.asarray(col_tiles, jnp.int32), h, w, *rope_in, *to_bf16)
    return outs[0], outs[1:]


def _merge_kernel(ya_ref, wa_ref, yb_ref, wb_ref, ga_ref, gb_ref, o_ref):
    a = jnp.dot(ya_ref[...], wa_ref[...], preferred_element_type=F32)
    b = jnp.dot(yb_ref[...], wb_ref[...], preferred_element_type=F32)
    ga = ga_ref[...].astype(F32)
    gb = gb_ref[...].astype(F32)
    o_ref[...] = (a / (1.0 + jnp.exp(-ga)) + b / (1.0 + jnp.exp(-gb))).astype(o_ref.dtype)


def _gated_merge(ya, wa, yb, wb, proj, ga_col, gb_col, *, bm=1024, bn=1024):
    m = ya.shape[0]
    n = wa.shape[1]
    ga_blk, gb_blk = ga_col // bn, gb_col // bn
    return pl.pallas_call(
        _merge_kernel,
        grid=(m // bm, n // bn),
        in_specs=[pl.BlockSpec((bm, ya.shape[1]), lambda i, j: (i, 0)),
                  pl.BlockSpec((wa.shape[0], bn), lambda i, j: (0, j)),
                  pl.BlockSpec((bm, yb.shape[1]), lambda i, j: (i, 0)),
                  pl.BlockSpec((wb.shape[0], bn), lambda i, j: (0, j)),
                  pl.BlockSpec((bm, bn), lambda i, j: (i, ga_blk + j)),
                  pl.BlockSpec((bm, bn), lambda i, j: (i, gb_blk + j))],
        out_specs=pl.BlockSpec((bm, bn), lambda i, j: (i, j)),
        out_shape=jax.ShapeDtypeStruct((m, n), BF16),
        compiler_params=_params("arbitrary", "arbitrary"),
        name="gated_merge",
    )(ya, wa, yb, wb, proj, proj)


def _residual_mm_kernel(a_ref, w_ref, x_ref, g_ref, o_ref):
    acc = jnp.dot(a_ref[...], w_ref[...], preferred_element_type=F32)
    o_ref[...] = x_ref[...] + g_ref[...] * acc


def _residual_matmul(a, w, x2d, gate, seq, *, bm=1024, bn=1024):
    m, k = a.shape
    n = w.shape[1]
    bsz = gate.shape[0]
    per_seq = seq // bm
    return pl.pallas_call(
        _residual_mm_kernel,
        grid=(m // bm, n // bn),
        in_specs=[pl.BlockSpec((bm, k), lambda i, j: (i, 0)),
                  pl.BlockSpec((k, bn), lambda i, j: (0, j)),
                  pl.BlockSpec((bm, bn), lambda i, j: (i, j)),
                  pl.BlockSpec((None, 1, bn), lambda i, j: (i // per_seq, 0, j))],
        out_specs=pl.BlockSpec((bm, bn), lambda i, j: (i, j)),
        out_shape=jax.ShapeDtypeStruct((m, n), F32),
        compiler_params=_params("arbitrary", "arbitrary"),
        name="out_projection",
    )(a, w, x2d, gate.reshape(bsz, 1, n))


def _relu2_mm_kernel(a_ref, w_ref, *rest, n_cast):
    o_ref = rest[n_cast]
    _side_casts(rest, n_cast)
    acc = jnp.dot(a_ref[...], w_ref[...], preferred_element_type=F32)
    r = jnp.maximum(acc, 0.0)
    o_ref[...] = (r * r).astype(o_ref.dtype)


def _relu2_matmul(a, w, *, to_bf16=(), bm=1024, bn=1024):
    m, k = a.shape
    n = w.shape[1]
    nj = n // bn
    cast_specs = [_side_cast_spec(mat, (m // bm) * nj, lambda i, j: i * nj + j) for mat in to_bf16]
    outs = pl.pallas_call(
        functools.partial(_relu2_mm_kernel, n_cast=len(to_bf16)),
        grid=(m // bm, nj),
        in_specs=[pl.BlockSpec((bm, k), lambda i, j: (i, 0)),
                  pl.BlockSpec((k, bn), lambda i, j: (0, j))] + cast_specs,
        out_specs=[pl.BlockSpec((bm, bn), lambda i, j: (i, j))] + cast_specs,
        out_shape=[jax.ShapeDtypeStruct((m, n), BF16)]
                  + [jax.ShapeDtypeStruct(mat.shape, BF16) for mat in to_bf16],
        compiler_params=_params("arbitrary", "arbitrary"),
        name="mlp_up",
    )(a, w, *to_bf16)
    return outs[0], outs[1:]


def _residual_mm_ksplit_kernel(a_ref, w_ref, x_ref, g_ref, o_ref):
    kk = pl.program_id(2)

    @pl.when(kk == 0)
    def _():
        o_ref[...] = jnp.zeros_like(o_ref)

    o_ref[...] += jnp.dot(a_ref[...], w_ref[...], preferred_element_type=F32)

    @pl.when(kk == pl.num_programs(2) - 1)
    def _():
        o_ref[...] = x_ref[...] + g_ref[...] * o_ref[...]


def _residual_matmul_ksplit(a, w, x2d, gate, seq, *, bm=1024, bn=1024, bk=4096):
    m, k = a.shape
    n = w.shape[1]
    bsz = gate.shape[0]
    per_seq = seq // bm
    return pl.pallas_call(
        _residual_mm_ksplit_kernel,
        grid=(m // bm, n // bn, k // bk),
        in_specs=[pl.BlockSpec((bm, bk), lambda i, j, kk: (i, kk)),
                  pl.BlockSpec((bk, bn), lambda i, j, kk: (kk, j)),
                  pl.BlockSpec((bm, bn), lambda i, j, kk: (i, j)),
                  pl.BlockSpec((None, 1, bn), lambda i, j, kk: (i // per_seq, 0, j))],
        out_specs=pl.BlockSpec((bm, bn), lambda i, j, kk: (i, j)),
        out_shape=jax.ShapeDtypeStruct((m, n), F32),
        compiler_params=_params("arbitrary", "arbitrary", "arbitrary"),
        name="mlp_down",
    )(a, w, x2d, gate.reshape(bsz, 1, n))


def _suffix_sum_matrix():
    r = np.arange(2 * LANES)[:, None] % LANES
    c = np.arange(2 * LANES)[None, :]
    return jnp.asarray(np.where(c < LANES, r >= c, True), dtype=BF16)


def _sb_kernel(q_ref, k_ref, v_ref, ut_ref, o_ref, acc_ref, carry_ref, *, tq, tk, scale):
    seq = q_ref.shape[0]
    n_heads = q_ref.shape[1] // HEAD_DIM
    n_diag = tq // tk
    z_scale = scale * LOG2_E

    def visit(hd, q0, r0, nr, k0, triangle):
        hcols = slice(hd * HEAD_DIM, (hd + 1) * HEAD_DIM)
        rows = slice(r0, r0 + nr)
        q = q_ref[pl.ds(pl.multiple_of(q0 + r0, tk), nr), hcols]
        k = k_ref[pl.ds(k0, tk), hcols]
        v = v_ref[pl.ds(k0, tk), hcols]
        z2 = lax.dot_general(q, k, (((1,), (1,)), ((), ())), preferred_element_type=F32) * z_scale
        nlr = jnp.maximum(z2, 0.0) + jnp.log2(1.0 + jnp.exp2(-jnp.abs(z2)))
        if triangle:
            row = lax.broadcasted_iota(jnp.int32, (nr, tk), 0)
            col = lax.broadcasted_iota(jnp.int32, (nr, tk), 1)
            causal = col < row
            nlr = jnp.where(causal, nlr, 0.0)
        carry = carry_ref[hd, rows, :]
        parts = [None] * (tk // LANES)
        for sb in reversed(range(tk // LANES)):
            cols = slice(sb * LANES, (sb + 1) * LANES)
            x = nlr[:, cols]
            hi = x.astype(BF16)
            lo = (x - hi.astype(F32)).astype(BF16)
            cs = jnp.dot(jnp.concatenate([hi, lo], axis=1), ut_ref[...], preferred_element_type=F32)
            a = jnp.exp2(z2[:, cols] - cs[:, :LANES] - carry)
            if triangle:
                a = jnp.where(causal[:, cols], a, 0.0)
            parts[sb] = a.astype(BF16)
            carry = carry + cs[:, LANES:]
        carry_ref[hd, rows, :] = carry
        acc_ref[hd, rows, :] += jnp.dot(jnp.concatenate(parts, axis=1), v, preferred_element_type=F32)

    def q_block(iq, _):
        q0 = pl.multiple_of(iq * tq, tq)
        acc_ref[...] = jnp.zeros_like(acc_ref)
        carry_ref[...] = jnp.zeros_like(carry_ref)
        for d in reversed(range(n_diag)):
            for hd in range(n_heads):
                visit(hd, q0, d * tk, tk, q0 + d * tk, True)
                if d < n_diag - 1:
                    visit(hd, q0, (d + 1) * tk, tq - (d + 1) * tk, q0 + d * tk, False)

        def more(state):
            t, lowest = state
            return jnp.logical_and(t < iq * n_diag, lowest < SATURATED)

        def left(state):
            t, _ = state
            k0 = pl.multiple_of(q0 - (t + 1) * tk, tk)
            for hd in range(n_heads):
                visit(hd, q0, 0, tq, k0, False)
            return t + 1, jnp.min(carry_ref[...])

        lax.while_loop(more, left, (jnp.int32(0), jnp.min(carry_ref[...])))
        for hd in range(n_heads):
            o_ref[pl.ds(q0, tq), hd * HEAD_DIM:(hd + 1) * HEAD_DIM] = acc_ref[hd].astype(o_ref.dtype)
        return 0

    lax.fori_loop(0, seq // tq, q_block, 0)


def _stick_breaking(proj, bsz, seq, n_heads, q_col, k_col, v_col, *, tq=512, tk=256, heads_per_step=4):
    hps = min(heads_per_step, n_heads)
    wb = hps * HEAD_DIM
    qb, kb, vb = q_col // wb, k_col // wb, v_col // wb
    kern = functools.partial(_sb_kernel, tq=tq, tk=tk, scale=HEAD_DIM ** -0.5)
    return pl.pallas_call(
        kern,
        grid=(bsz, n_heads // hps),
        in_specs=[pl.BlockSpec((seq, wb), lambda b, h: (b, qb + h)),
                  pl.BlockSpec((seq, wb), lambda b, h: (b, kb + h)),
                  pl.BlockSpec((seq, wb), lambda b, h: (b, vb + h)),
                  pl.BlockSpec((2 * LANES, 2 * LANES), lambda b, h: (0, 0))],
        out_specs=pl.BlockSpec((seq, wb), lambda b, h: (b, h)),
        out_shape=jax.ShapeDtypeStruct((bsz * seq, n_heads * HEAD_DIM), BF16),
        scratch_shapes=[pltpu.VMEM((hps, tq, HEAD_DIM), F32), pltpu.VMEM((hps, tq, LANES), F32)],
        compiler_params=_params("arbitrary", "arbitrary"),
        name="stick_breaking",
    )(proj, proj, proj, _suffix_sum_matrix())


WINDOW = 128
DIL_TILE = 1024
MIN_PROBLEM_ROWS = 256


def _window_softmax(qs, ks, vs, valid, scale):
    scores = []
    for q, k in zip(qs, ks):
        s = lax.dot_general(q, k, (((1,), (1,)), ((), ())), preferred_element_type=F32) * scale
        scores.append(jnp.where(valid, s, -jnp.inf))
    stats = []
    for s in scores:
        m = jnp.max(s, axis=1, keepdims=True)
        p = jnp.exp(s - m)
        stats.append((p.astype(BF16), m, jnp.sum(p, axis=1, keepdims=True)))
    return [(jnp.dot(p, v, preferred_element_type=F32) / l, m + jnp.log(l))
            for (p, m, l), v in zip(stats, vs)]


def _dil_strided_kernel(q_ref, k_ref, v_ref, o_ref, lse_ref, qd_ref, kd_ref, vd_ref, *, dil, group, scale):
    n_heads, tile, _ = q_ref.shape
    chunk = tile // dil
    n_hist = -(-WINDOW // chunk)
    n_slots = n_hist + 1
    takes = [min(chunk, WINDOW - (back - 1) * chunk) for back in range(1, n_hist + 1)]
    assert group == 1 or all(t == chunk for t in takes)
    seg = group * chunk
    n_keys = group * (WINDOW + chunk)
    kt = pl.program_id(1)
    slot = lax.rem(kt, n_slots)

    @pl.when(kt == 0)
    def _():
        kd_ref[...] = jnp.zeros_like(kd_ref)
        vd_ref[...] = jnp.zeros_like(vd_ref)

    def deinterleave(c, _):
        rows = pl.ds(pl.multiple_of(c * chunk, chunk), chunk)
        for h in range(n_heads):
            qd_ref[h, rows, :] = q_ref[h, pl.ds(c, chunk, stride=dil), :].astype(BF16)
            kd_ref[slot, h, rows, :] = k_ref[h, pl.ds(c, chunk, stride=dil), :].astype(BF16)
            vd_ref[slot, h, rows, :] = v_ref[h, pl.ds(c, chunk, stride=dil), :].astype(BF16)
        return 0

    lax.fori_loop(0, dil, deinterleave, 0)

    row = lax.broadcasted_iota(jnp.int32, (seg, n_keys), 0)
    col = lax.broadcasted_iota(jnp.int32, (seg, n_keys), 1)
    if group == 1:
        i, u, same_class = row, col, None
    else:
        lc, ls = chunk.bit_length() - 1, seg.bit_length() - 1
        i = row & (chunk - 1)
        within = col & (seg - 1)
        u = ((col >> ls) << lc) + (within & (chunk - 1))
        same_class = (within >> lc) == (row >> lc)
    valid = jnp.logical_and(u >= i, u <= i + WINDOW)
    valid = jnp.logical_and(valid, u >= WINDOW - chunk * kt)
    if same_class is not None:
        valid = jnp.logical_and(valid, same_class)
    lane = lax.broadcasted_iota(jnp.int32, (seg, LANES), 1)

    def attend(gi, _):
        base = pl.multiple_of(gi * seg, seg)
        qs, ks, vs = [], [], []
        for h in range(n_heads):
            kparts, vparts = [], []
            for back in range(n_hist, 0, -1):
                take = takes[back - 1]
                old = lax.rem(kt + (n_slots - back), n_slots)
                rows = pl.ds(pl.multiple_of(base + (chunk - take), 16), take if group == 1 else seg)
                kparts.append(kd_ref[old, h, rows, :])
                vparts.append(vd_ref[old, h, rows, :])
            kparts.append(kd_ref[slot, h, pl.ds(base, seg), :])
            vparts.append(vd_ref[slot, h, pl.ds(base, seg), :])
            qs.append(qd_ref[h, pl.ds(base, seg), :])
            ks.append(jnp.concatenate(kparts, axis=0))
            vs.append(jnp.concatenate(vparts, axis=0))
        lse_blk = jnp.zeros((seg, LANES), F32)
        for h, (o, lse) in enumerate(_window_softmax(qs, ks, vs, valid, scale)):
            for cc in range(group):
                o_ref[h, pl.ds(gi * group + cc, chunk, stride=dil), :] = o[cc * chunk:(cc + 1) * chunk]
            lse_blk = jnp.where(lane == h, lse, lse_blk)
        for cc in range(group):
            lse_ref[pl.ds(gi * group + cc, chunk, stride=dil), :] = lse_blk[cc * chunk:(cc + 1) * chunk]
        return 0

    lax.fori_loop(0, dil // group, attend, 0)


def _dil_strided(heads_qk, heads_v, g, bsz, seq, n_heads):
    n = heads_qk.shape[1]
    dil = DIL_GROUPS[g][1]
    tile = min(DIL_TILE, seq)
    chunk = tile // dil
    n_slots = -(-WINDOW // chunk) + 1
    group = max(1, min(dil, MIN_PROBLEM_ROWS // chunk))
    per_seq = seq // tile

    def operand(slab_block):
        return pl.BlockSpec((n_heads, tile, HEAD_DIM), lambda b, t: (slab_block, b * per_seq + t, 0))

    return pl.pallas_call(
        functools.partial(_dil_strided_kernel, dil=dil, group=group, scale=HEAD_DIM ** -0.5),
        grid=(bsz, per_seq),
        in_specs=[operand(g - 1), operand(2 + g - 1), operand(g - 1)],
        out_specs=[pl.BlockSpec((n_heads, tile, HEAD_DIM), lambda b, t: (0, b * per_seq + t, 0)),
                   pl.BlockSpec((tile, LANES), lambda b, t: (b * per_seq + t, 0))],
        out_shape=[jax.ShapeDtypeStruct((n_heads, n, HEAD_DIM), F32),
                   jax.ShapeDtypeStruct((n, LANES), F32)],
        scratch_shapes=[pltpu.VMEM((n_heads, tile, HEAD_DIM), BF16),
                        pltpu.VMEM((n_slots, n_heads, tile, HEAD_DIM), BF16),
                        pltpu.VMEM((n_slots, n_heads, tile, HEAD_DIM), BF16)],
        compiler_params=_params("arbitrary", "arbitrary"),
        name=f"dilated_stride{dil}",
    )(heads_qk, heads_qk, heads_v)


def _dil_merge_kernel(q_ref, kp_ref, kc_ref, vp_ref, vc_ref, o2_ref, l2_ref, o3_ref, l3_ref, y_ref,
                      *, scale):
    tile = q_ref.shape[0]
    n_heads = q_ref.shape[1] // HEAD_DIM
    blk = WINDOW
    first = pl.program_id(1) == 0
    row = lax.broadcasted_iota(jnp.int32, (blk, 2 * blk), 0)
    col = lax.broadcasted_iota(jnp.int32, (blk, 2 * blk), 1)
    band = jnp.logical_and(col >= row, col <= row + blk)
    band_first = jnp.logical_and(band, jnp.logical_or(col >= blk, jnp.logical_not(first)))
    for sub in range(tile // blk):
        rows = slice(sub * blk, (sub + 1) * blk)
        heads = [slice(h * HEAD_DIM, (h + 1) * HEAD_DIM) for h in range(n_heads)]
        if sub == 0:
            ks = [jnp.concatenate([kp_ref[:, cols], kc_ref[rows, cols]], axis=0) for cols in heads]
            vs = [jnp.concatenate([vp_ref[:, cols], vc_ref[rows, cols]], axis=0) for cols in heads]
            valid = band_first
        else:
            ks = [kc_ref[(sub - 1) * blk:(sub + 1) * blk, cols] for cols in heads]
            vs = [vc_ref[(sub - 1) * blk:(sub + 1) * blk, cols] for cols in heads]
            valid = band
        for h in range(n_heads):
            (o1, lse1), = _window_softmax([q_ref[rows, heads[h]]], ks[h:h + 1], vs[h:h + 1], valid, scale)
            lse2 = l2_ref[rows, h:h + 1]
            lse3 = l3_ref[rows, h:h + 1]
            mx = jnp.maximum(jnp.maximum(lse1, lse2), lse3)
            w1, w2, w3 = jnp.exp(lse1 - mx), jnp.exp(lse2 - mx), jnp.exp(lse3 - mx)
            y = (w1 * o1 + w2 * o2_ref[h, rows, :] + w3 * o3_ref[h, rows, :]) / (w1 + w2 + w3)
            y_ref[rows, heads[h]] = y.astype(y_ref.dtype)


def _dil_merge(qk, q_col, k_col, proj, v_col, o2, l2, o3, l3, bsz, seq, n_heads, *, tile=512):
    n = proj.shape[0]
    gw = n_heads * HEAD_DIM
    tile = min(tile, seq)
    per_seq = seq // tile
    sub_per_tile = tile // WINDOW
    sub_per_seq = seq // WINDOW
    qo, ko, vo = q_col // gw, k_col // gw, v_col // gw

    def cur(off):
        return pl.BlockSpec((tile, gw), lambda b, t: (b * per_seq + t, off))

    def prev(off):
        return pl.BlockSpec(
            (WINDOW, gw), lambda b, t: (b * sub_per_seq + jnp.maximum(t * sub_per_tile - 1, 0), off))

    heads = pl.BlockSpec((n_heads, tile, HEAD_DIM), lambda b, t: (0, b * per_seq + t, 0))
    lses = pl.BlockSpec((tile, LANES), lambda b, t: (b * per_seq + t, 0))
    return pl.pallas_call(
        functools.partial(_dil_merge_kernel, scale=HEAD_DIM ** -0.5),
        grid=(bsz, per_seq),
        in_specs=[cur(qo), prev(ko), cur(ko), prev(vo), cur(vo), heads, lses, heads, lses],
        out_specs=pl.BlockSpec((tile, gw), lambda b, t: (b * per_seq + t, 0)),
        out_shape=jax.ShapeDtypeStruct((n, gw), BF16),
        compiler_params=_params("arbitrary", "arbitrary"),
        name="dilated_merge",
    )(qk, qk, qk, proj, proj, o2, l2, o3, l3)


def _dilated_attention(qk, q_col, k_col, proj, v_col, heads_qk, heads_v, bsz, seq, n_heads):
    o3, l3 = _dil_strided(heads_qk, heads_v, 2, bsz, seq, n_heads)
    o2, l2 = _dil_strided(heads_qk, heads_v, 1, bsz, seq, n_heads)
    return _dil_merge(qk, q_col, k_col, proj, v_col, o2, l2, o3, l3, bsz, seq, n_heads)


def kernel(x, c, positions, ada_w, ada_b, norm_mix_w, w_in, w_o_sb, w_o_dil, w_out,
           norm_mlp_w, w_ff1, w_ff2, norm_final_w):
    bsz, seq, d = x.shape
    n = bsz * seq
    d_sb = w_o_sb.shape[1]
    d_dil = w_o_dil.shape[1]
    sb_heads = d_sb // HEAD_DIM
    dil_heads = d_dil // HEAD_DIM
    bn = d_dil
    sb_t = 3 * d_sb // bn
    dil_t = lambda which, g: sb_t + which * N_DIL + g
    gate_t = sb_t + 3 * N_DIL
    plain_tiles = list(range(sb_t)) + [dil_t(2, 0)] + list(range(gate_t, gate_t + 2 * d // bn))
    v0_col = sb_t * bn
    g_sb_col = v0_col + bn
    g_dil_col = g_sb_col + d
    qk0_tiles = [dil_t(0, 0), dil_t(1, 0)]
    heads_qk_tiles = [dil_t(which, g) for which in range(2) for g in (1, 2)]
    heads_v_tiles = [dil_t(2, g) for g in (1, 2)]

    cos, sin = _rope_tables(positions)
    xf = x.reshape(n, d)
    for l in range(ada_w.shape[0]):
        mod = _ada_modulation(c, ada_w[l], ada_b[l])
        sh1, sc1, g1, sh2, sc2, g2 = [mod[:, i * d:(i + 1) * d] for i in range(6)]

        h = _norm_mod(xf, norm_mix_w[l], sc1, sh1, seq)
        w_in_l = w_in[l].astype(BF16)
        proj, (w_ff1_l, w_out_l, w_o_sb_l, w_o_dil_l) = _in_projection(
            h, w_in_l, plain_tiles, head_major=False, bn=bn,
            to_bf16=(w_ff1[l], w_out[l], w_o_sb[l], w_o_dil[l]), name="in_projection")
        qk0, _ = _in_projection(h, w_in_l, qk0_tiles, head_major=False, rope_tables=(cos, sin), bn=bn,
                                name="in_projection_rotary")
        heads_qk, _ = _in_projection(h, w_in_l, heads_qk_tiles, head_major=True, rope_tables=(cos, sin),
                                     bn=bn, name="in_projection_heads_rotary")
        heads_v, _ = _in_projection(h, w_in_l, heads_v_tiles, head_major=True, bn=bn,
                                    name="in_projection_heads")
        y_a = _stick_breaking(proj, bsz, seq, sb_heads, 0, d_sb, 2 * d_sb)
        y_b = _dilated_attention(qk0, 0, bn, proj, v0_col, heads_qk, heads_v, bsz, seq, dil_heads)
        merged = _gated_merge(y_a, w_o_sb_l, y_b, w_o_dil_l, proj, g_sb_col, g_dil_col)
        xf = _residual_matmul(merged, w_out_l, xf, g1, seq)

        h = _norm_mod(xf, norm_mlp_w[l], sc2, sh2, seq)
        u, (w_ff2_l,) = _relu2_matmul(h, w_ff1_l, to_bf16=(w_ff2[l],))
        xf = _residual_matmul_ksplit(u, w_ff2_l, xf, g2, seq)
    return _final_norm(xf, norm_final_w).reshape(bsz, seq, d)
```
